```python
import math
import jax
import jax.numpy as jnp
from jax import lax
import numpy as np

D_MODEL = 1024
BATCH = 4
SEQ = 4096
DEPTH = 1
DEC_BATCH = 128
DEC_SEQ = 8
PAST_LEN = 16384
PAGE_SIZE = 128

SSM_WIDTH = D_MODEL
SSM_GROUP = 16
SSM_GROUPS = SSM_WIDTH // SSM_GROUP
SSM_STATE = 64
SSM_DT_MIN = 0.001
SSM_DT_MAX = 0.1
N_HEADS = 16
QK_NOPE = 64
QK_ROPE = 32
V_DIM = 64
Q_LORA = 384
KV_LORA = 256
ATTN_WIDTH = N_HEADS * V_DIM
ROPE_BASE = 10000.0
ATTN_SCALE = (QK_NOPE + QK_ROPE) ** -0.5
Q_BLOCK = 128
NEG_INF = -1e30
NORM_EPS = 1e-6
IN_WIDTHS = (SSM_WIDTH, SSM_WIDTH, Q_LORA, KV_LORA, QK_ROPE, ATTN_WIDTH, D_MODEL, D_MODEL)
IN_COLS = SSM_WIDTH * 2 + Q_LORA + KV_LORA + QK_ROPE + ATTN_WIDTH + 2 * D_MODEL

kernel_name = "hybrid_s5_mla_gated_step"


def rmsnorm(x, g):
    xf = x.astype(jnp.float32)
    y = xf * lax.rsqrt(jnp.mean(xf * xf, axis=-1, keepdims=True) + NORM_EPS)
    return (y * g.astype(jnp.float32)).astype(x.dtype)


def split_in(z):
    bounds, acc = [], 0
    for w in IN_WIDTHS[:-1]:
        acc += w
        bounds.append(acc)
    return jnp.split(z, bounds, axis=-1)


def rope_cos_sin(pos):
    half = QK_ROPE // 2
    inv = ROPE_BASE ** (-jnp.arange(half, dtype=jnp.float32) * (2.0 / QK_ROPE))
    ang = pos.astype(jnp.float32)[:, None] * inv[None, :]
    return jnp.cos(ang), jnp.sin(ang)


def apply_rope(x, cos, sin):
    half = QK_ROPE // 2
    xf = x.astype(jnp.float32)
    x1, x2 = xf[..., :half], xf[..., half:]
    return jnp.concatenate([x1 * cos - x2 * sin, x1 * sin + x2 * cos], axis=-1).astype(x.dtype)


def _linear_combine(e1, e2):
    a1, b1 = e1
    a2, b2 = e2
    return a1 * a2, a2 * b1 + b2


def s5_branch(u, h0, w):
    B, L, _ = u.shape
    f32 = jnp.float32
    lam = lax.complex(w["ssm_lambda_re"].astype(f32), w["ssm_lambda_im"].astype(f32))
    dt = jnp.exp(w["ssm_log_dt"].astype(f32))[:, None]
    lam_bar = jnp.exp(lam * dt)
    b = lax.complex(w["ssm_b_re"].astype(f32), w["ssm_b_im"].astype(f32))
    b_bar = ((lam_bar - 1.0) / lam)[..., None] * b
    c = lax.complex(w["ssm_c_re"].astype(f32), w["ssm_c_im"].astype(f32))
    uf = u.astype(f32)
    ug = uf.reshape(B, L, SSM_GROUPS, SSM_GROUP).astype(jnp.complex64)
    bu = jnp.einsum('gpc,blgc->blgp', b_bar, ug)
    bu = bu.at[:, 0].add(lam_bar[None] * h0)
    a = jnp.broadcast_to(lam_bar, bu.shape)
    _, h = lax.associative_scan(_linear_combine, (a, bu), axis=1)
    y = jnp.real(jnp.einsum('gcp,blgp->blgc', c, h)).reshape(B, L, SSM_WIDTH)
    y = y + w["ssm_d"].astype(f32) * uf
    zg = jax.nn.gelu(y)
    out = zg * jax.nn.sigmoid(zg @ w["ssm_w_glu"].astype(f32) + w["ssm_b_glu"].astype(f32))
    return out.astype(u.dtype), h[:, -1]


def latent_scores(q_lat, q_rope, ckv, k_rope):
    s = jnp.einsum('bqhc,bkc->bhqk', q_lat, ckv) + jnp.einsum('bqhr,bkr->bhqk', q_rope, k_rope)
    return s.astype(jnp.float32) * ATTN_SCALE


def prompt_attention(q_lat, q_rope, ckv, k_rope):
    B, L = q_lat.shape[:2]
    nb = L // Q_BLOCK
    ql = q_lat.reshape(B, nb, Q_BLOCK, N_HEADS, KV_LORA).transpose(1, 0, 2, 3, 4)
    qr = q_rope.reshape(B, nb, Q_BLOCK, N_HEADS, QK_ROPE).transpose(1, 0, 2, 3, 4)
    kpos = jnp.arange(L)

    def block(args):
        i, qlb, qrb = args
        s = latent_scores(qlb, qrb, ckv, k_rope)
        qpos = i * Q_BLOCK + jnp.arange(Q_BLOCK)
        s = jnp.where(kpos[None, :] <= qpos[:, None], s, NEG_INF)
        p = jax.nn.softmax(s, axis=-1)
        return jnp.einsum('bhqk,bkc->bqhc', p.astype(ckv.dtype), ckv)

    o = lax.map(block, (jnp.arange(nb), ql, qr))
    return o.transpose(1, 0, 2, 3, 4).reshape(B, L, N_HEADS, KV_LORA)


def paged_attention(q_lat, q_rope, ckv, k_rope, cache_ckv, cache_krope, page_table):
    f32 = jnp.float32
    T = q_lat.shape[1]
    s = latent_scores(q_lat, q_rope, ckv, k_rope)
    s = jnp.where(jnp.tril(jnp.ones((T, T), dtype=bool)), s, NEG_INF)
    m = jnp.max(s, axis=-1)
    p = jnp.exp(s - m[..., None])
    l = jnp.sum(p, axis=-1)
    acc = jnp.einsum('bhqk,bkc->bhqc', p, ckv.astype(f32))

    def page_step(carry, pages):
        m, l, acc = carry
        kc = cache_ckv[pages]
        kr = cache_krope[pages]
        sp = latent_scores(q_lat, q_rope, kc, kr)
        m_new = jnp.maximum(m, jnp.max(sp, axis=-1))
        corr = jnp.exp(m - m_new)
        pp = jnp.exp(sp - m_new[..., None])
        l = l * corr + jnp.sum(pp, axis=-1)
        acc = acc * corr[..., None] + jnp.einsum('bhqk,bkc->bhqc', pp, kc.astype(f32))
        return (m_new, l, acc), None

    (m, l, acc), _ = lax.scan(page_step, (m, l, acc), page_table.T)
    o = acc / l[..., None]
    return o.transpose(0, 2, 1, 3).astype(q_lat.dtype)


def hybrid_layer(x, pos, ssm_h0, attend, w):
    B, L, _ = x.shape
    xn = rmsnorm(x, w["norm_in"])
    z = xn @ w["w_in"]
    u_s, g_s, cq, ckv_raw, kr_raw, g_a, m_s, m_a = split_in(z)
    y_s, h_last = s5_branch(u_s, ssm_h0, w)
    y_s = (y_s * jax.nn.silu(g_s)) @ w["w_br_ssm"]
    cos, sin = rope_cos_sin(pos)
    cq = rmsnorm(cq, w["mla_q_norm"])
    q = jnp.einsum('blr,rhd->blhd', cq, w["mla_w_uq"])
    q_rope = apply_rope(q[..., QK_NOPE:], cos[:, None, :], sin[:, None, :])
    q_lat = jnp.einsum('blhd,chd->blhc', q[..., :QK_NOPE], w["mla_w_uk"])
    ckv = rmsnorm(ckv_raw, w["mla_kv_norm"])
    k_rope = apply_rope(kr_raw, cos, sin)
    o_lat = attend(q_lat, q_rope, ckv, k_rope)
    o = jnp.einsum('blhc,chd->blhd', o_lat, w["mla_w_uv"]).reshape(B, L, ATTN_WIDTH)
    y_a = (o * jax.nn.silu(g_a)) @ w["w_br_attn"]
    merged = jax.nn.sigmoid(m_s) * y_s + jax.nn.sigmoid(m_a) * y_a
    h = x + merged @ w["w_out"]
    return h, ckv, k_rope, h_last


def setup_inputs(seed: int = 0) -> dict:
    key = jax.random.key(seed)
    ks = jax.random.split(key, 32)
    f32 = jnp.float32
    n_pages = PAST_LEN // PAGE_SIZE
    n_used = DEC_BATCH * n_pages
    n_pool = n_used + n_used // 4

    def nrm(k, shape, scale):
        return jax.random.normal(k, shape, f32) * scale

    G, P, GC = SSM_GROUPS, SSM_STATE, SSM_GROUP
    return {
        "x_prompt": nrm(ks[0], (BATCH, SEQ, D_MODEL), 1.0),
        "x_sample": nrm(ks[1], (DEC_BATCH, DEC_SEQ, D_MODEL), 1.0),
        "cache_ckv": nrm(ks[2], (n_pool, PAGE_SIZE, KV_LORA), 1.0),
        "cache_krope": nrm(ks[3], (n_pool, PAGE_SIZE, QK_ROPE), 1.0),
        "state_ssm": nrm(ks[4], (DEC_BATCH, G, P, 2), 0.5),
        "page_table": jax.random.permutation(ks[5], n_pool)[:n_used].reshape(DEC_BATCH, n_pages).astype(jnp.int32),
        "norm_in": 1.0 + nrm(ks[6], (D_MODEL,), 0.02),
        "w_in": nrm(ks[7], (D_MODEL, IN_COLS), D_MODEL ** -0.5),
        "ssm_lambda_re": -0.5 + nrm(ks[8], (G, P), 0.01),
        "ssm_lambda_im": math.pi * jnp.arange(P, dtype=f32)[None, :] + nrm(ks[9], (G, P), 0.01),
        "ssm_log_dt": jax.random.uniform(ks[10], (G,), f32, math.log(SSM_DT_MIN), math.log(SSM_DT_MAX)),
        "ssm_b_re": nrm(ks[11], (G, P, GC), (2 * GC) ** -0.5),
        "ssm_b_im": nrm(ks[12], (G, P, GC), (2 * GC) ** -0.5),
        "ssm_c_re": nrm(ks[13], (G, GC, P), (2 * P) ** -0.5),
        "ssm_c_im": nrm(ks[14], (G, GC, P), (2 * P) ** -0.5),
        "ssm_d": nrm(ks[15], (SSM_WIDTH,), 0.5),
        "ssm_w_glu": nrm(ks[16], (SSM_WIDTH, SSM_WIDTH), SSM_WIDTH ** -0.5),
        "ssm_b_glu": nrm(ks[17], (SSM_WIDTH,), 0.02),
        "w_br_ssm": nrm(ks[18], (SSM_WIDTH, D_MODEL), SSM_WIDTH ** -0.5),
        "mla_q_norm": 1.0 + nrm(ks[19], (Q_LORA,), 0.02),
        "mla_w_uq": nrm(ks[20], (Q_LORA, N_HEADS, QK_NOPE + QK_ROPE), Q_LORA ** -0.5),
        "mla_kv_norm": 1.0 + nrm(ks[21], (KV_LORA,), 0.02),
        "mla_w_uk": nrm(ks[22], (KV_LORA, N_HEADS, QK_NOPE), KV_LORA ** -0.5),
        "mla_w_uv": nrm(ks[23], (KV_LORA, N_HEADS, V_DIM), KV_LORA ** -0.5),
        "w_br_attn": nrm(ks[24], (ATTN_WIDTH, D_MODEL), ATTN_WIDTH ** -0.5),
        "w_out": nrm(ks[25], (D_MODEL, D_MODEL), D_MODEL ** -0.5),
        "norm_final": 1.0 + nrm(ks[26], (D_MODEL,), 0.02),
    }


def reference(x_prompt, x_sample, cache_ckv, cache_krope, state_ssm, page_table,
              norm_in, w_in, ssm_lambda_re, ssm_lambda_im, ssm_log_dt, ssm_b_re, ssm_b_im,
              ssm_c_re, ssm_c_im, ssm_d, ssm_w_glu, ssm_b_glu, w_br_ssm,
              mla_q_norm, mla_w_uq, mla_kv_norm, mla_w_uk, mla_w_uv, w_br_attn, w_out, norm_final):
    w = dict(norm_in=norm_in, w_in=w_in, ssm_lambda_re=ssm_lambda_re, ssm_lambda_im=ssm_lambda_im,
             ssm_log_dt=ssm_log_dt, ssm_b_re=ssm_b_re, ssm_b_im=ssm_b_im, ssm_c_re=ssm_c_re,
             ssm_c_im=ssm_c_im, ssm_d=ssm_d, ssm_w_glu=ssm_w_glu, ssm_b_glu=ssm_b_glu,
             w_br_ssm=w_br_ssm, mla_q_norm=mla_q_norm, mla_w_uq=mla_w_uq, mla_kv_norm=mla_kv_norm,
             mla_w_uk=mla_w_uk, mla_w_uv=mla_w_uv, w_br_attn=w_br_attn, w_out=w_out)
    B, L, _ = x_prompt.shape
    Bd, T, _ = x_sample.shape
    h0_p = jnp.zeros((B, SSM_GROUPS, SSM_STATE), jnp.complex64)
    h_p, ckv_p, krope_p, hs_p = hybrid_layer(x_prompt, jnp.arange(L), h0_p, prompt_attention, w)
    h0_s = lax.complex(state_ssm[..., 0].astype(jnp.float32), state_ssm[..., 1].astype(jnp.float32))
    attend_s = lambda ql, qr, ck, kr: paged_attention(ql, qr, ck, kr, cache_ckv, cache_krope, page_table)
    h_s, ckv_s, krope_s, hs_s = hybrid_layer(x_sample, PAST_LEN + jnp.arange(T), h0_s, attend_s, w)
    y_prompt = rmsnorm(h_p, norm_final)
    y_sample = rmsnorm(h_s, norm_final)
    ssm_p = jnp.stack([jnp.real(hs_p), jnp.imag(hs_p)], axis=-1)
    ssm_s = jnp.stack([jnp.real(hs_s), jnp.imag(hs_s)], axis=-1)
    return (y_prompt, y_sample, ckv_p, krope_p, ssm_p, ckv_s, krope_s, ssm_s)
```

```python
import functools
import math

import jax
import jax.numpy as jnp
from jax import lax
from jax.experimental import pallas as pl
from jax.experimental.pallas import tpu as pltpu

D_MODEL = 1024
PAST_LEN = 16384
SSM_GROUP = 16
SSM_GROUPS = 64
SSM_STATE = 64
N_HEADS = 16
QK_NOPE = 64
QK_ROPE = 32
V_DIM = 64
Q_LORA = 384
KV_LORA = 256
ATTN_WIDTH = N_HEADS * V_DIM
ROPE_BASE = 10000.0
ATTN_SCALE = (QK_NOPE + QK_ROPE) ** -0.5
NEG_INF = -1e30
NORM_EPS = 1e-6

LANE = 128
ROPE_PAD = LANE
QK_WIDTH = KV_LORA + ROPE_PAD
STATE_LANES = 2 * SSM_STATE
VMEM_LIMIT = 56 * 1024 * 1024

F32 = jnp.float32
BF16 = jnp.bfloat16

_C_U = 0
_C_GS = _C_U + D_MODEL
_C_CQ = _C_GS + D_MODEL
_C_CKV = _C_CQ + Q_LORA
_C_KR = _C_CKV + KV_LORA
_C_KRS = _C_KR + ROPE_PAD
_C_GA = _C_KRS + ROPE_PAD
_C_MS = _C_GA + ATTN_WIDTH
_C_MA = _C_MS + D_MODEL
_C_END = _C_MA + D_MODEL


def _rms(x, g):
    return x * lax.rsqrt(jnp.mean(x * x, axis=-1, keepdims=True) + NORM_EPS) * g


def _dot(a, b):
    return jnp.dot(a, b, preferred_element_type=F32)


def _dot_nt(a, b):
    return lax.dot_general(a, b, (((1,), (1,)), ((), ())), preferred_element_type=F32)


def _params(*sem):
    return pltpu.CompilerParams(dimension_semantics=sem, vmem_limit_bytes=VMEM_LIMIT)


def _in_proj_kernel(x_ref, gin_ref, w_ref, gkv_ref, cos_ref, sin_ref,
                    u_ref, gs_ref, cq_ref, ckv_ref, kr_ref, kf_ref, ga_ref, ms_ref, ma_ref):
    xn = _rms(x_ref[...], gin_ref[...]).astype(BF16)

    def proj(lo, hi):
        return _dot(xn, w_ref[:, lo:hi])

    u_ref[...] = proj(_C_U, _C_GS)
    gs_ref[...] = proj(_C_GS, _C_CQ)
    cq_ref[...] = proj(_C_CQ, _C_CKV)
    ckv = _rms(proj(_C_CKV, _C_KR), gkv_ref[...])
    ckv_ref[...] = ckv
    kr = proj(_C_KR, _C_KRS) * cos_ref[...] + proj(_C_KRS, _C_GA) * sin_ref[...]
    kr_ref[...] = kr[:, :QK_ROPE]
    kf_ref[:, :KV_LORA] = ckv.astype(BF16)
    kf_ref[:, KV_LORA:] = kr.astype(BF16)
    ga_ref[...] = proj(_C_GA, _C_MS)
    ms_ref[...] = proj(_C_MS, _C_MA)
    ma_ref[...] = proj(_C_MA, _C_END)


def _in_proj(x, gin, w_all, gkv, cos_k, sin_k, tm):
    n = x.shape[0]
    nblk = cos_k.shape[0] // tm
    row = lambda w: pl.BlockSpec((tm, w), lambda i: (i, 0))
    full = lambda a: pl.BlockSpec(a.shape, lambda i: (0,) * a.ndim)
    tab = pl.BlockSpec((tm, ROPE_PAD), lambda i: (i % nblk, 0))
    widths = (D_MODEL, D_MODEL, Q_LORA, KV_LORA, QK_ROPE, QK_WIDTH, ATTN_WIDTH, D_MODEL, D_MODEL)
    dtypes = (F32, F32, F32, F32, F32, BF16, F32, F32, F32)
    return pl.pallas_call(
        _in_proj_kernel,
        grid=(n // tm,),
        in_specs=[row(D_MODEL), full(gin), full(w_all), full(gkv), tab, tab],
        out_specs=[row(w) for w in widths],
        out_shape=[jax.ShapeDtypeStruct((n, w), d) for w, d in zip(widths, dtypes)],
        compiler_params=_params("parallel"),
        name="in_proj",
    )(x, gin, w_all, gkv, cos_k, sin_k)


def _q_prep_kernel(cq_ref, gq_ref, wn_ref, wr_ref, wrs_ref, wuk_ref, cos_ref, sin_ref, q_ref):
    nb, _, t, _ = q_ref.shape
    cqn = _rms(cq_ref[...], gq_ref[...]).astype(BF16)
    qn = _dot(cqn, wn_ref[...]).astype(BF16)
    qr = _dot(cqn, wr_ref[...])
    qrs = _dot(cqn, wrs_ref[...])
    cos = cos_ref[...]
    sin = sin_ref[...]
    for h in range(N_HEADS):
        sl = slice(h * LANE, (h + 1) * LANE)
        q_lat = _dot(qn[:, sl], wuk_ref[h])
        q_rope = qr[:, sl] * cos + qrs[:, sl] * sin
        q_ref[:, h, :, :KV_LORA] = q_lat.reshape(nb, t, KV_LORA).astype(q_ref.dtype)
        q_ref[:, h, :, KV_LORA:] = q_rope.reshape(nb, t, ROPE_PAD).astype(q_ref.dtype)


def _q_prep(cq, gq, wn, wr, wrs, wuk, cos_q, sin_q, tm, batch, seq, out_dtype):
    n = cq.shape[0]
    nblk = cos_q.shape[0] // tm
    full = lambda a: pl.BlockSpec(a.shape, lambda i: (0,) * a.ndim)
    tab = pl.BlockSpec((tm, ROPE_PAD), lambda i: (i % nblk, 0))
    if seq >= tm:
        per = seq // tm
        out_spec = pl.BlockSpec((1, N_HEADS, tm, QK_WIDTH), lambda i: (i // per, 0, i % per, 0))
    else:
        out_spec = pl.BlockSpec((tm // seq, N_HEADS, seq, QK_WIDTH), lambda i: (i, 0, 0, 0))
    return pl.pallas_call(
        _q_prep_kernel,
        grid=(n // tm,),
        in_specs=[pl.BlockSpec((tm, Q_LORA), lambda i: (i, 0)), full(gq), full(wn), full(wr), full(wrs),
                  full(wuk), tab, tab],
        out_specs=out_spec,
        out_shape=jax.ShapeDtypeStruct((batch, N_HEADS, seq, QK_WIDTH), out_dtype),
        compiler_params=_params("parallel"),
        name="q_prep",
    )(cq, gq, wn, wr, wrs, wuk, cos_q, sin_q)


def _softmax_step(s, v, m_ref, l_ref, acc_ref):
    m_prev = m_ref[...]
    m_new = jnp.maximum(m_prev, jnp.max(s, axis=-1, keepdims=True))
    alpha = jnp.exp(m_prev - m_new)
    p = jnp.exp(s - m_new)
    l_ref[...] = alpha * l_ref[...] + jnp.sum(p, axis=-1, keepdims=True)
    acc_ref[...] = alpha * acc_ref[...] + _dot(p.astype(BF16), v)
    m_ref[...] = m_new


def _value_up(o_lat, wuv_ref, t):
    cols = []
    for j in range(N_HEADS // 2):
        a = o_lat[(2 * j) * t:(2 * j + 1) * t].astype(BF16)
        b = o_lat[(2 * j + 1) * t:(2 * j + 2) * t].astype(BF16)
        cols.append(_dot(a, wuv_ref[2 * j]) + _dot(b, wuv_ref[2 * j + 1]))
    return jnp.concatenate(cols, axis=-1)


def _init_softmax(m_ref, l_ref, acc_ref):
    m_ref[...] = jnp.full(m_ref.shape, -jnp.inf, F32)
    l_ref[...] = jnp.zeros(l_ref.shape, F32)
    acc_ref[...] = jnp.zeros(acc_ref.shape, F32)


def _prompt_attn_kernel(q_ref, k_ref, wuv_ref, o_ref, m_ref, l_ref, acc_ref, *, tq):
    qi = pl.program_id(1)
    rows = N_HEADS * tq
    q = q_ref[0].reshape(rows, QK_WIDTH)
    _init_softmax(m_ref, l_ref, acc_ref)

    def scores(kb):
        k = k_ref[0, pl.ds(pl.multiple_of(kb * tq, tq), tq), :]
        return _dot_nt(q, k) * ATTN_SCALE, k[:, :KV_LORA]

    def body(kb, carry):
        s, v = scores(kb)
        _softmax_step(s, v, m_ref, l_ref, acc_ref)
        return carry

    lax.fori_loop(0, qi, body, 0)
    s, v = scores(qi)
    causal = (lax.broadcasted_iota(jnp.int32, (tq, tq), 1) <= lax.broadcasted_iota(jnp.int32, (tq, tq), 0))
    s = jnp.where(causal[None], s.reshape(N_HEADS, tq, tq), NEG_INF).reshape(rows, tq)
    _softmax_step(s, v, m_ref, l_ref, acc_ref)
    o_ref[0] = _value_up(acc_ref[...] / l_ref[...], wuv_ref, tq)


def _prompt_attn(q, kf, wuv, tq):
    b, _, l, _ = q.shape
    return pl.pallas_call(
        functools.partial(_prompt_attn_kernel, tq=tq),
        grid=(b, l // tq),
        in_specs=[pl.BlockSpec((1, N_HEADS, tq, QK_WIDTH), lambda bi, qi: (bi, 0, qi, 0)),
                  pl.BlockSpec((1, l, QK_WIDTH), lambda bi, qi: (bi, 0, 0)),
                  pl.BlockSpec(wuv.shape, lambda bi, qi: (0, 0, 0))],
        out_specs=pl.BlockSpec((1, tq, ATTN_WIDTH), lambda bi, qi: (bi, qi, 0)),
        out_shape=jax.ShapeDtypeStruct((b, l, ATTN_WIDTH), F32),
        scratch_shapes=[pltpu.VMEM((N_HEADS * tq, 1), F32), pltpu.VMEM((N_HEADS * tq, 1), F32),
                        pltpu.VMEM((N_HEADS * tq, KV_LORA), F32)],
        compiler_params=_params("parallel", "arbitrary"),
        name="prompt_attn",
    )(q, kf, wuv)


def _paged_attn_kernel(pt_ref, q_ref, ks_ref, rs_ref, wuv_ref, *rest, pps, page, t):
    ckv_refs = rest[:pps]
    kr_refs = rest[pps:2 * pps]
    o_ref, m_ref, l_ref, acc_ref, kbuf, rbuf = rest[2 * pps:]
    del pt_ref
    j = pl.program_id(1)
    rows = N_HEADS * t
    q = q_ref[0].astype(BF16)
    q_lat = q[:, :KV_LORA]
    q_rope = q[:, KV_LORA:]
    rbuf[...] = jnp.zeros(rbuf.shape, BF16)

    @pl.when(j == 0)
    def _():
        _init_softmax(m_ref, l_ref, acc_ref)
        k = jnp.concatenate([ks_ref[0], jnp.zeros((page - t, KV_LORA), F32)], axis=0).astype(BF16)
        r = jnp.concatenate([rs_ref[0], jnp.zeros((page - t, QK_ROPE), F32)], axis=0).astype(BF16)
        rbuf[:page, :QK_ROPE] = r
        s = (_dot_nt(q_lat, k) + _dot_nt(q_rope, rbuf[:page, :])) * ATTN_SCALE
        tok = lax.broadcasted_iota(jnp.int32, (t, page), 0)
        key = lax.broadcasted_iota(jnp.int32, (t, page), 1)
        s = jnp.where((key <= tok)[None], s.reshape(N_HEADS, t, page), NEG_INF).reshape(rows, page)
        _softmax_step(s, k, m_ref, l_ref, acc_ref)

    for i in range(pps):
        kbuf[i * page:(i + 1) * page, :] = ckv_refs[i][0].astype(BF16)
        rbuf[i * page:(i + 1) * page, :QK_ROPE] = kr_refs[i][0].astype(BF16)
    k = kbuf[...]
    s = (_dot_nt(q_lat, k) + _dot_nt(q_rope, rbuf[...])) * ATTN_SCALE
    _softmax_step(s, k, m_ref, l_ref, acc_ref)

    @pl.when(j == pl.num_programs(1) - 1)
    def _():
        o_ref[0] = _value_up(acc_ref[...] / l_ref[...], wuv_ref, t)


def _paged_attn(q, ckv_new, kr_new, cache_ckv, cache_kr, page_table, wuv, pps):
    bd, rows, _ = q.shape
    t = rows // N_HEADS
    _, page, _ = cache_ckv.shape
    n_pages = page_table.shape[1]
    assert n_pages % pps == 0 and t <= page
    page_spec = lambda w, i: pl.BlockSpec((1, page, w), lambda b, j, pt: (pt[b, j * pps + i], 0, 0))
    grid_spec = pltpu.PrefetchScalarGridSpec(
        num_scalar_prefetch=1,
        grid=(bd, n_pages // pps),
        in_specs=[pl.BlockSpec((1, rows, QK_WIDTH), lambda b, j, pt: (b, 0, 0)),
                  pl.BlockSpec((1, t, KV_LORA), lambda b, j, pt: (b, 0, 0)),
                  pl.BlockSpec((1, t, QK_ROPE), lambda b, j, pt: (b, 0, 0)),
                  pl.BlockSpec(wuv.shape, lambda b, j, pt: (0, 0, 0))]
                 + [page_spec(KV_LORA, i) for i in range(pps)]
                 + [page_spec(QK_ROPE, i) for i in range(pps)],
        out_specs=pl.BlockSpec((1, t, ATTN_WIDTH), lambda b, j, pt: (b, 0, 0)),
        scratch_shapes=[pltpu.VMEM((rows, 1), F32), pltpu.VMEM((rows, 1), F32),
                        pltpu.VMEM((rows, KV_LORA), F32),
                        pltpu.VMEM((pps * page, KV_LORA), BF16), pltpu.VMEM((pps * page, ROPE_PAD), BF16)],
    )
    return pl.pallas_call(
        functools.partial(_paged_attn_kernel, pps=pps, page=page, t=t),
        grid_spec=grid_spec,
        out_shape=jax.ShapeDtypeStruct((bd, t, ATTN_WIDTH), F32),
        compiler_params=_params("parallel", "arbitrary"),
        name="paged_attn",
    )(page_table, q, ckv_new, kr_new, wuv, *([cache_ckv] * pps), *([cache_kr] * pps))


def _cmul(x, pw):
    return x * pw[0:1] + pltpu.roll(x, SSM_STATE, axis=1) * pw[1:2]


def _s5_kernel(*refs, nc, nb, has_h0):
    if has_h0:
        u_ref, mg_ref, bz_ref, cz_ref, pw_ref, h0_ref, y_ref, hl_ref = refs
    else:
        u_ref, mg_ref, bz_ref, cz_ref, pw_ref, y_ref, hl_ref = refs
    u = u_ref[0]
    y = _dot(u, mg_ref[0])
    s = _dot(u, bz_ref[0])
    rows = s.shape[0]
    if has_h0:
        assert nc == 1
        s_prev = h0_ref[0]
        s = s + _cmul(s_prev, pw_ref[0, 0])
    else:
        chunk = lax.broadcasted_iota(jnp.int32, (rows, STATE_LANES), 0) & (nc - 1)
        for k in range(nc.bit_length() - 1):
            sh = 1 << k
            shifted = jnp.where(chunk >= sh, pltpu.roll(s, sh, axis=0), 0.0)
            s = s + _cmul(shifted, pw_ref[0, k])
        s_prev = jnp.where(chunk >= 1, pltpu.roll(s, 1, axis=0), 0.0)
    y_ref[0] = y + _dot(s_prev.astype(BF16), cz_ref[0])
    if nc == 1:
        hl_ref[0] = s
    else:
        for b in range(nb):
            hl_ref[0, b:b + 1, :] = s[b * nc + nc - 1:b * nc + nc, :]


def _s5(u_t, mg, bz, cz, pw, h0, nc, nb):
    g, rows, kdim = u_t.shape
    blk = lambda a: pl.BlockSpec((1,) + a.shape[1:], lambda i: (i,) + (0,) * (a.ndim - 1))
    ins = [u_t, mg, bz, cz, pw] + ([h0] if h0 is not None else [])
    return pl.pallas_call(
        functools.partial(_s5_kernel, nc=nc, nb=nb, has_h0=h0 is not None),
        grid=(g,),
        in_specs=[blk(a) for a in ins],
        out_specs=[pl.BlockSpec((1, rows, kdim), lambda i: (i, 0, 0)),
                   pl.BlockSpec((1, nb, STATE_LANES), lambda i: (i, 0, 0))],
        out_shape=[jax.ShapeDtypeStruct((g, rows, kdim), F32), jax.ShapeDtypeStruct((g, nb, STATE_LANES), F32)],
        compiler_params=_params("parallel"),
        name="s5",
    )(*ins)


def _s5_weights(lam_re, lam_im, log_dt, b_re, b_im, c_re, c_im, tc, n_levels):
    hi = lax.Precision.HIGHEST
    lam = lax.complex(lam_re.astype(F32), lam_im.astype(F32))
    ldt = lam * jnp.exp(log_dt.astype(F32))[:, None]
    lam_bar = jnp.exp(ldt)
    b_bar = ((lam_bar - 1.0) / lam)[..., None] * lax.complex(b_re.astype(F32), b_im.astype(F32))
    c = lax.complex(c_re.astype(F32), c_im.astype(F32))
    steps = jnp.arange(tc + 1, dtype=F32)
    pw = jnp.exp(ldt[None] * steps[:, None, None])
    kern = jnp.real(jnp.einsum('gop,dgp,gpi->dgoi', c, pw[:tc], b_bar, precision=hi))
    lag = jnp.arange(tc)[None, :] - jnp.arange(tc)[:, None]
    mg = jnp.where((lag >= 0)[:, :, None, None, None], kern[jnp.clip(lag, 0)], 0.0)
    mg = mg.transpose(2, 0, 4, 1, 3).reshape(SSM_GROUPS, tc * SSM_GROUP, tc * SSM_GROUP)
    bzc = pw[tc - 1 - jnp.arange(tc)][..., None] * b_bar[None]
    bzc = bzc.transpose(1, 0, 3, 2).reshape(SSM_GROUPS, tc * SSM_GROUP, SSM_STATE)
    bz = jnp.concatenate([jnp.real(bzc), jnp.imag(bzc)], axis=-1)
    czc = c[None] * pw[1:tc + 1][:, :, None, :]
    czc = czc.transpose(1, 3, 0, 2).reshape(SSM_GROUPS, SSM_STATE, tc * SSM_GROUP)
    cz = jnp.concatenate([jnp.real(czc), -jnp.imag(czc)], axis=1)
    lev = jnp.exp(ldt[None] * (tc * 2.0 ** jnp.arange(n_levels, dtype=F32))[:, None, None])
    re, im = jnp.real(lev), jnp.imag(lev)
    pwl = jnp.stack([jnp.concatenate([re, re], -1), jnp.concatenate([-im, im], -1)], axis=2)
    return mg.astype(BF16), bz.astype(BF16), cz.astype(BF16), pwl.transpose(1, 0, 2, 3)


def _s5_branch(u, h0, ssm_w, batch, seq, tc):
    nc = seq // tc
    n_levels = max(nc.bit_length() - 1, 1)
    assert nc & (nc - 1) == 0 and (h0 is None or nc == 1)
    mg, bz, cz, pwl = _s5_weights(*ssm_w, tc, n_levels)
    u_t = u.astype(BF16).reshape(batch, nc, tc, SSM_GROUPS, SSM_GROUP).transpose(3, 0, 1, 2, 4)
    u_t = u_t.reshape(SSM_GROUPS, batch * nc, tc * SSM_GROUP)
    h0_t = None
    if h0 is not None:
        h0_t = jnp.concatenate([h0[..., 0], h0[..., 1]], axis=-1).astype(F32).transpose(1, 0, 2)
    y_t, hl = _s5(u_t, mg, bz, cz, pwl, h0_t, nc, batch)
    y = y_t.reshape(SSM_GROUPS, batch, nc, tc, SSM_GROUP).transpose(1, 2, 3, 0, 4).reshape(batch * seq, -1)
    hl = hl.transpose(1, 0, 2)
    state = jnp.stack([hl[..., :SSM_STATE], hl[..., SSM_STATE:]], axis=-1)
    return y, state


def _merge_kernel(x_ref, y_ref, u_ref, gs_ref, o_ref, ga_ref, ms_ref, ma_ref,
                  d_ref, wglu_ref, bglu_ref, wbs_ref, wba_ref, wout_ref, gf_ref, out_ref):
    y = y_ref[...] + d_ref[...] * u_ref[...]
    zg = jax.nn.gelu(y)
    glu = zg * jax.nn.sigmoid(_dot(zg.astype(BF16), wglu_ref[...]) + bglu_ref[...])
    y_s = _dot((glu * jax.nn.silu(gs_ref[...])).astype(BF16), wbs_ref[...])
    y_a = _dot((o_ref[...] * jax.nn.silu(ga_ref[...])).astype(BF16), wba_ref[...])
    merged = jax.nn.sigmoid(ms_ref[...]) * y_s + jax.nn.sigmoid(ma_ref[...]) * y_a
    h = x_ref[...] + _dot(merged.astype(BF16), wout_ref[...])
    out_ref[...] = _rms(h, gf_ref[...])


def _merge(x, y, u, gs, o, ga, ms, ma, d, wglu, bglu, wbs, wba, wout, gf, tm):
    n = x.shape[0]
    row = pl.BlockSpec((tm, D_MODEL), lambda i: (i, 0))
    full = lambda a: pl.BlockSpec(a.shape, lambda i: (0,) * a.ndim)
    consts = (d, wglu, bglu, wbs, wba, wout, gf)
    return pl.pallas_call(
        _merge_kernel,
        grid=(n // tm,),
        in_specs=[row] * 8 + [full(a) for a in consts],
        out_specs=row,
        out_shape=jax.ShapeDtypeStruct((n, D_MODEL), F32),
        compiler_params=_params("parallel"),
        name="merge",
    )(x, y, u, gs, o, ga, ms, ma, *consts)


def _rope_tables(pos, reps):
    half = QK_ROPE // 2
    inv = ROPE_BASE ** (-jnp.arange(half, dtype=F32) * (2.0 / QK_ROPE))
    ang = pos.astype(F32)[:, None] * inv[None, :]
    cos, sin = jnp.cos(ang), jnp.sin(ang)
    pad = jnp.zeros((pos.shape[0], ROPE_PAD - QK_ROPE), F32)
    cos_t = jnp.concatenate([cos, cos, pad], axis=-1)
    sin_t = jnp.concatenate([-sin, sin, pad], axis=-1)
    return jnp.tile(cos_t, (reps, 1)), jnp.tile(sin_t, (reps, 1))


def _swap_halves(w):
    half = QK_ROPE // 2
    return jnp.concatenate([w[..., half:], w[..., :half]], axis=-1)


def _pad_last(w, width):
    return jnp.pad(w, [(0, 0)] * (w.ndim - 1) + [(0, width - w.shape[-1])])


def _pack_weights(w_in, mla_w_uq, mla_w_uk, mla_w_uv):
    b = [0]
    for w in (D_MODEL, D_MODEL, Q_LORA, KV_LORA, QK_ROPE, ATTN_WIDTH, D_MODEL, D_MODEL):
        b.append(b[-1] + w)
    w_kr = w_in[:, b[4]:b[5]]
    w_all = jnp.concatenate([w_in[:, :b[4]], _pad_last(w_kr, ROPE_PAD), _pad_last(_swap_halves(w_kr), ROPE_PAD),
                             w_in[:, b[5]:]], axis=-1).astype(BF16)
    w_nope = _pad_last(mla_w_uq[..., :QK_NOPE], LANE).reshape(Q_LORA, N_HEADS * LANE).astype(BF16)
    w_rope = mla_w_uq[..., QK_NOPE:]
    w_r = _pad_last(w_rope, ROPE_PAD).reshape(Q_LORA, N_HEADS * ROPE_PAD).astype(BF16)
    w_rs = _pad_last(_swap_halves(w_rope), ROPE_PAD).reshape(Q_LORA, N_HEADS * ROPE_PAD).astype(BF16)
    w_uk = jnp.pad(mla_w_uk.transpose(1, 2, 0), ((0, 0), (0, LANE - QK_NOPE), (0, 0))).astype(BF16)
    w_uv = mla_w_uv.transpose(1, 0, 2)
    lo = jnp.pad(w_uv, ((0, 0), (0, 0), (0, LANE - V_DIM)))
    hi = jnp.pad(w_uv, ((0, 0), (0, 0), (LANE - V_DIM, 0)))
    w_uv = jnp.where((jnp.arange(N_HEADS) % 2 == 0)[:, None, None], lo, hi).astype(BF16)
    return w_all, w_nope, w_r, w_rs, w_uk, w_uv


def _layer(x, pos, h0, attend, w, batch, seq, tc, tm, q_dtype):
    n = x.shape[0]
    reps = max(tm // seq, 1)
    cos_t, sin_t = _rope_tables(pos, reps)
    row = lambda v: v.reshape(1, -1).astype(F32)
    u, gs, cq, ckv, kr, kf, ga, ms, ma = _in_proj(x, row(w["norm_in"]), w["w_all"], row(w["mla_kv_norm"]),
                                                  cos_t, sin_t, tm)
    q = _q_prep(cq, row(w["mla_q_norm"]), w["w_nope"], w["w_r"], w["w_rs"], w["w_uk"], cos_t, sin_t,
                tm, batch, seq, q_dtype)
    o = attend(q, ckv, kr, kf).reshape(n, ATTN_WIDTH)
    y_ssm, state = _s5_branch(u, h0, w["ssm"], batch, seq, tc)
    y = _merge(x, y_ssm, u, gs, o, ga, ms, ma, row(w["ssm_d"]), w["ssm_w_glu"].astype(BF16), row(w["ssm_b_glu"]),
               w["w_br_ssm"].astype(BF16), w["w_br_attn"].astype(BF16), w["w_out"].astype(BF16),
               row(w["norm_final"]), tm)
    return y, ckv, kr, state


def kernel(x_prompt, x_sample, cache_ckv, cache_krope, state_ssm, page_table,
           norm_in, w_in, ssm_lambda_re, ssm_lambda_im, ssm_log_dt, ssm_b_re, ssm_b_im,
           ssm_c_re, ssm_c_im, ssm_d, ssm_w_glu, ssm_b_glu, w_br_ssm,
           mla_q_norm, mla_w_uq, mla_kv_norm, mla_w_uk, mla_w_uv, w_br_attn, w_out, norm_final):
    b, l, _ = x_prompt.shape
    bd, t, _ = x_sample.shape
    w_all, w_nope, w_r, w_rs, w_uk, w_uv = _pack_weights(w_in, mla_w_uq, mla_w_uk, mla_w_uv)
    w = dict(norm_in=norm_in, w_all=w_all, mla_kv_norm=mla_kv_norm, mla_q_norm=mla_q_norm,
             w_nope=w_nope, w_r=w_r, w_rs=w_rs, w_uk=w_uk,
             ssm=(ssm_lambda_re, ssm_lambda_im, ssm_log_dt, ssm_b_re, ssm_b_im, ssm_c_re, ssm_c_im),
             ssm_d=ssm_d, ssm_w_glu=ssm_w_glu, ssm_b_glu=ssm_b_glu, w_br_ssm=w_br_ssm,
             w_br_attn=w_br_attn, w_out=w_out, norm_final=norm_final)
    tm = 256
    tq = 256

    def attend_prompt(q, ckv, kr, kf):
        return _prompt_attn(q, kf.reshape(b, l, QK_WIDTH), w_uv, tq)

    def attend_sample(q, ckv, kr, kf):
        return _paged_attn(q.reshape(bd, N_HEADS * t, QK_WIDTH), ckv.reshape(bd, t, KV_LORA),
                           kr.reshape(bd, t, QK_ROPE), cache_ckv, cache_krope, page_table, w_uv, pps=16)

    y_p, ckv_p, kr_p, ssm_p = _layer(x_prompt.reshape(b * l, D_MODEL), jnp.arange(l), None, attend_prompt,
                                     w, b, l, 16, tm, BF16)
    y_s, ckv_s, kr_s, ssm_s = _layer(x_sample.reshape(bd * t, D_MODEL), PAST_LEN + jnp.arange(t), state_ssm,
                                     attend_sample, w, bd, t, t, tm, F32)
    return (y_p.reshape(b, l, D_MODEL), y_s.reshape(bd, t, D_MODEL),
            ckv_p.reshape(b, l, KV_LORA), kr_p.reshape(b, l, QK_ROPE), ssm_p,
            ckv_s.reshape(bd, t, KV_LORA), kr_s.reshape(bd, t, QK_ROPE), ssm_s)
```

```python
import functools
import math

import jax
import jax.numpy as jnp
from jax import lax
from jax.experimental import pallas as pl
from jax.experimental.pallas import tpu as pltpu

D_MODEL = 1024
PAST_LEN = 16384
SSM_GROUP = 16
SSM_GROUPS = 64
SSM_STATE = 64
N_HEADS = 16
QK_NOPE = 64
QK_ROPE = 32
V_DIM = 64
Q_LORA = 384
KV_LORA = 256
ATTN_WIDTH = N_HEADS * V_DIM
ROPE_BASE = 10000.0
ATTN_SCALE = (QK_NOPE + QK_ROPE) ** -0.5
NEG_INF = -1e30
NORM_EPS = 1e-6

LANE = 128
ROPE_PAD = LANE
QK_WIDTH = KV_LORA + ROPE_PAD
STATE_LANES = 2 * SSM_STATE
VMEM_LIMIT = 56 * 1024 * 1024

F32 = jnp.float32
BF16 = jnp.bfloat16

_C_U = 0
_C_GS = _C_U + D_MODEL
_C_CQ = _C_GS + D_MODEL
_C_CKV = _C_CQ + Q_LORA
_C_KR = _C_CKV + KV_LORA
_C_KRS = _C_KR + ROPE_PAD
_C_GA = _C_KRS + ROPE_PAD
_C_MS = _C_GA + ATTN_WIDTH
_C_MA = _C_MS + D_MODEL
_C_END = _C_MA + D_MODEL


def _rms(x, g):
    return x * lax.rsqrt(jnp.mean(x * x, axis=-1, keepdims=True) + NORM_EPS) * g


def _dot(a, b):
    return jnp.dot(a, b, preferred_element_type=F32)


def _dot_nt(a, b):
    return lax.dot_general(a, b, (((1,), (1,)), ((), ())), preferred_element_type=F32)


def _params(*sem):
    return pltpu.CompilerParams(dimension_semantics=sem, vmem_limit_bytes=VMEM_LIMIT)


def _in_proj_kernel(x_ref, gin_ref, w_ref, gkv_ref, cos_ref, sin_ref,
                    u_ref, gs_ref, cq_ref, ckv_ref, kr_ref, kf_ref, vt_ref, ga_ref, ms_ref, ma_ref):
    xn = _rms(x_ref[...], gin_ref[...]).astype(BF16)

    def proj(lo, hi):
        return _dot(xn, w_ref[:, lo:hi])

    u_ref[...] = proj(_C_U, _C_GS)
    gs_ref[...] = proj(_C_GS, _C_CQ)
    cq_ref[...] = proj(_C_CQ, _C_CKV)
    ckv = _rms(proj(_C_CKV, _C_KR), gkv_ref[...])
    ckv_ref[...] = ckv
    kr = proj(_C_KR, _C_KRS) * cos_ref[...] + proj(_C_KRS, _C_GA) * sin_ref[...]
    kr_ref[...] = kr[:, :QK_ROPE]
    kf_ref[:, :KV_LORA] = ckv.astype(BF16)
    kf_ref[:, KV_LORA:] = kr.astype(BF16)
    vt_ref[0] = ckv.T.astype(BF16)
    ga_ref[...] = proj(_C_GA, _C_MS)
    ms_ref[...] = proj(_C_MS, _C_MA)
    ma_ref[...] = proj(_C_MA, _C_END)


def _in_proj(x, gin, w_all, gkv, cos_k, sin_k, tm):
    n = x.shape[0]
    nblk = cos_k.shape[0] // tm
    row = lambda w: pl.BlockSpec((tm, w), lambda i: (i, 0))
    full = lambda a: pl.BlockSpec(a.shape, lambda i: (0,) * a.ndim)
    tab = pl.BlockSpec((tm, ROPE_PAD), lambda i: (i % nblk, 0))
    widths = (D_MODEL, D_MODEL, Q_LORA, KV_LORA, QK_ROPE, QK_WIDTH, None, ATTN_WIDTH, D_MODEL, D_MODEL)
    dtypes = (F32, F32, F32, F32, F32, BF16, BF16, F32, F32, F32)
    vt_spec = pl.BlockSpec((1, KV_LORA, tm), lambda i: (i, 0, 0))
    shape = lambda w: (n, w) if w else (n // tm, KV_LORA, tm)
    return pl.pallas_call(
        _in_proj_kernel,
        grid=(n // tm,),
        in_specs=[row(D_MODEL), full(gin), full(w_all), full(gkv), tab, tab],
        out_specs=[row(w) if w else vt_spec for w in widths],
        out_shape=[jax.ShapeDtypeStruct(shape(w), d) for w, d in zip(widths, dtypes)],
        compiler_params=_params("parallel"),
        name="in_proj",
    )(x, gin, w_all, gkv, cos_k, sin_k)


def _q_prep_kernel(cq_ref, gq_ref, wn_ref, wr_ref, wrs_ref, wuk_ref, cos_ref, sin_ref, q_ref):
    nb, _, t, _ = q_ref.shape
    cqn = _rms(cq_ref[...], gq_ref[...]).astype(BF16)
    qn = _dot(cqn, wn_ref[...]).astype(BF16)
    qr = _dot(cqn, wr_ref[...])
    qrs = _dot(cqn, wrs_ref[...])
    cos = cos_ref[...]
    sin = sin_ref[...]
    for h in range(N_HEADS):
        sl = slice(h * LANE, (h + 1) * LANE)
        q_lat = _dot(qn[:, sl], wuk_ref[h])
        q_rope = qr[:, sl] * cos + qrs[:, sl] * sin
        q_ref[:, h, :, :KV_LORA] = q_lat.reshape(nb, t, KV_LORA).astype(q_ref.dtype)
        q_ref[:, h, :, KV_LORA:] = q_rope.reshape(nb, t, ROPE_PAD).astype(q_ref.dtype)


def _q_prep(cq, gq, wn, wr, wrs, wuk, cos_q, sin_q, tm, batch, seq, out_dtype):
    n = cq.shape[0]
    nblk = cos_q.shape[0] // tm
    full = lambda a: pl.BlockSpec(a.shape, lambda i: (0,) * a.ndim)
    tab = pl.BlockSpec((tm, ROPE_PAD), lambda i: (i % nblk, 0))
    if seq >= tm:
        per = seq // tm
        out_spec = pl.BlockSpec((1, N_HEADS, tm, QK_WIDTH), lambda i: (i // per, 0, i % per, 0))
    else:
        out_spec = pl.BlockSpec((tm // seq, N_HEADS, seq, QK_WIDTH), lambda i: (i, 0, 0, 0))
    return pl.pallas_call(
        _q_prep_kernel,
        grid=(n // tm,),
        in_specs=[pl.BlockSpec((tm, Q_LORA), lambda i: (i, 0)), full(gq), full(wn), full(wr), full(wrs),
                  full(wuk), tab, tab],
        out_specs=out_spec,
        out_shape=jax.ShapeDtypeStruct((batch, N_HEADS, seq, QK_WIDTH), out_dtype),
        compiler_params=_params("parallel"),
        name="q_prep",
    )(cq, gq, wn, wr, wrs, wuk, cos_q, sin_q)


def _q_prep_t_kernel(cq_ref, gq_ref, wn_ref, wr_ref, wrs_ref, wuk_ref, cos_ref, sin_ref, q_ref):
    tm = cq_ref.shape[0]
    cqn = _rms(cq_ref[...], gq_ref[...]).astype(BF16)
    qn = _dot_nt(wn_ref[...], cqn).astype(BF16)
    qr = _dot_nt(wr_ref[...], cqn)
    qrs = _dot_nt(wrs_ref[...], cqn)
    cos = cos_ref[...]
    sin = sin_ref[...]
    for h in range(N_HEADS):
        sl = slice(h * LANE, (h + 1) * LANE)
        cl = slice(h * tm, (h + 1) * tm)
        q_ref[0, 0, :KV_LORA, cl] = _dot(wuk_ref[h], qn[sl]).astype(BF16)
        q_ref[0, 0, KV_LORA:, cl] = (qr[sl] * cos + qrs[sl] * sin).astype(BF16)


def _q_prep_t(cq, gq, wn_t, wr_t, wrs_t, wuk_t, cos_qt, sin_qt, tm, batch, seq):
    n = cq.shape[0]
    per = seq // tm
    full = lambda a: pl.BlockSpec(a.shape, lambda i: (0,) * a.ndim)
    tab = pl.BlockSpec((ROPE_PAD, tm), lambda i: (0, i % per))
    return pl.pallas_call(
        _q_prep_t_kernel,
        grid=(n // tm,),
        in_specs=[pl.BlockSpec((tm, Q_LORA), lambda i: (i, 0)), full(gq), full(wn_t), full(wr_t), full(wrs_t),
                  full(wuk_t), tab, tab],
        out_specs=pl.BlockSpec((1, 1, QK_WIDTH, N_HEADS * tm), lambda i: (i // per, i % per, 0, 0)),
        out_shape=jax.ShapeDtypeStruct((batch, per, QK_WIDTH, N_HEADS * tm), BF16),
        compiler_params=_params("parallel"),
        name="q_prep_t",
    )(cq, gq, wn_t, wr_t, wrs_t, wuk_t, cos_qt, sin_qt)


def _softmax_step(s, v, m_ref, l_ref, acc_ref):
    m_prev = m_ref[...]
    m_new = jnp.maximum(m_prev, jnp.max(s, axis=-1, keepdims=True))
    alpha = jnp.exp(m_prev - m_new)
    p = jnp.exp(s - m_new)
    l_ref[...] = alpha * l_ref[...] + jnp.sum(p, axis=-1, keepdims=True)
    acc_ref[...] = alpha * acc_ref[...] + _dot(p.astype(BF16), v)
    m_ref[...] = m_new


def _value_up(o_lat, wuv_ref, t):
    cols = []
    for j in range(N_HEADS // 2):
        a = o_lat[(2 * j) * t:(2 * j + 1) * t].astype(BF16)
        b = o_lat[(2 * j + 1) * t:(2 * j + 2) * t].astype(BF16)
        cols.append(_dot(a, wuv_ref[2 * j]) + _dot(b, wuv_ref[2 * j + 1]))
    return jnp.concatenate(cols, axis=-1)


def _init_softmax(m_ref, l_ref, acc_ref):
    m_ref[...] = jnp.full(m_ref.shape, -jnp.inf, F32)
    l_ref[...] = jnp.zeros(l_ref.shape, F32)
    acc_ref[...] = jnp.zeros(acc_ref.shape, F32)


def _prompt_attn_kernel(qt_ref, k_ref, vt_ref, wuvt_ref, o_ref, m_ref, l_ref, acc_ref, *, tq, cg):
    qi = pl.program_id(1)
    cols = N_HEADS * tq
    _init_softmax(m_ref, l_ref, acc_ref)

    def chunk(kb, diagonal):
        k = k_ref[0, pl.ds(pl.multiple_of(kb * tq, tq), tq), :]
        vt = vt_ref[kb]
        if diagonal:
            key = lax.broadcasted_iota(jnp.int32, (tq, cg), 0)
            tok = lax.broadcasted_iota(jnp.int32, (tq, cg), 1) & (tq - 1)
            causal = key <= tok
        for g in range(cols // cg):
            sl = slice(g * cg, (g + 1) * cg)
            s = _dot(k, qt_ref[0, 0, :, sl]) * ATTN_SCALE
            if diagonal:
                s = jnp.where(causal, s, NEG_INF)
            m_prev = m_ref[:, sl]
            m_new = jnp.maximum(m_prev, jnp.max(s, axis=0, keepdims=True))
            alpha = jnp.exp(m_prev - m_new)
            p = jnp.exp(s - m_new)
            l_ref[:, sl] = alpha * l_ref[:, sl] + jnp.sum(p, axis=0, keepdims=True)
            acc_ref[:, sl] = alpha * acc_ref[:, sl] + _dot(vt, p.astype(BF16))
            m_ref[:, sl] = m_new

    def body(kb, carry):
        chunk(kb, False)
        return carry

    lax.fori_loop(0, qi, body, 0)
    chunk(qi, True)
    heads = []
    for h in range(N_HEADS):
        sl = slice(h * tq, (h + 1) * tq)
        o_lat = (acc_ref[:, sl] / l_ref[:, sl]).astype(BF16)
        heads.append(_dot(wuvt_ref[h], o_lat))
    o_ref[0] = jnp.concatenate(heads, axis=0).T


def _prompt_attn(qt, kf, vt, wuv_t, tq, cg):
    b, nq, _, cols = qt.shape
    l = nq * tq
    return pl.pallas_call(
        functools.partial(_prompt_attn_kernel, tq=tq, cg=cg),
        grid=(b, nq),
        in_specs=[pl.BlockSpec((1, 1, QK_WIDTH, cols), lambda bi, qi: (bi, qi, 0, 0)),
                  pl.BlockSpec((1, l, QK_WIDTH), lambda bi, qi: (bi, 0, 0)),
                  pl.BlockSpec((nq, KV_LORA, tq), lambda bi, qi: (bi, 0, 0)),
                  pl.BlockSpec(wuv_t.shape, lambda bi, qi: (0, 0, 0))],
        out_specs=pl.BlockSpec((1, tq, ATTN_WIDTH), lambda bi, qi: (bi, qi, 0)),
        out_shape=jax.ShapeDtypeStruct((b, l, ATTN_WIDTH), F32),
        scratch_shapes=[pltpu.VMEM((1, cols), F32), pltpu.VMEM((1, cols), F32),
                        pltpu.VMEM((KV_LORA, cols), F32)],
        compiler_params=_params("parallel", "arbitrary"),
        name="prompt_attn",
    )(qt, kf, vt, wuv_t)


ROPE_PER_ROW = LANE // QK_ROPE


def _paged_attn_kernel(pt_ref, q_ref, ks_ref, rs_ref, wuv_ref, *rest, pps, page, t):
    ckv_refs = rest[:pps]
    kr_refs = rest[pps:2 * pps]
    o_ref, m_ref, l_ref, acc_ref, kbuf, rbuf = rest[2 * pps:]
    del pt_ref
    j = pl.program_id(1)
    rows = N_HEADS * t
    sub = page // ROPE_PER_ROW
    q = q_ref[0].astype(BF16)
    q_lat = q[:, :KV_LORA]
    q_rope = q[:, KV_LORA:]

    @pl.when(j == 0)
    def _():
        _init_softmax(m_ref, l_ref, acc_ref)
        k = jnp.concatenate([ks_ref[0], jnp.zeros((page - t, KV_LORA), F32)], axis=0).astype(BF16)
        r = jnp.concatenate([rs_ref[0], jnp.zeros((page - t, QK_ROPE), F32)], axis=0).astype(BF16)
        rbuf[:page, :] = jnp.zeros((page, ROPE_PAD), BF16)
        rbuf[:page, :QK_ROPE] = r
        s = (_dot_nt(q_lat, k) + _dot_nt(q_rope, rbuf[:page, :])) * ATTN_SCALE
        tok = lax.broadcasted_iota(jnp.int32, (t, page), 0)
        key = lax.broadcasted_iota(jnp.int32, (t, page), 1)
        s = jnp.where((key <= tok)[None], s.reshape(N_HEADS, t, page), NEG_INF).reshape(rows, page)
        _softmax_step(s, k, m_ref, l_ref, acc_ref)

    lane_k = lax.shift_right_logical(lax.broadcasted_iota(jnp.int32, (sub, LANE), 1), QK_ROPE.bit_length() - 1)
    for i in range(pps):
        dense = kr_refs[i][0]
        for kk in range(ROPE_PER_ROW):
            dst = pl.ds((i * ROPE_PER_ROW + kk) * sub, sub)
            for half in range(KV_LORA // LANE):
                src = pl.ds(kk * (KV_LORA // LANE) + half, sub, stride=ROPE_PER_ROW * (KV_LORA // LANE))
                kbuf[dst, half * LANE:(half + 1) * LANE] = ckv_refs[i][0, src, :].astype(BF16)
            rbuf[dst, :] = jnp.where(lane_k == kk, dense, 0.0).astype(BF16)
    k = kbuf[...]
    s = (_dot_nt(q_lat, k) + _dot_nt(q_rope, rbuf[...])) * ATTN_SCALE
    _softmax_step(s, k, m_ref, l_ref, acc_ref)

    @pl.when(j == pl.num_programs(1) - 1)
    def _():
        o_ref[0] = _value_up(acc_ref[...] / l_ref[...], wuv_ref, t)


def _paged_attn(q, ckv_new, kr_new, cache_ckv, cache_kr, page_table, wuv, pps):
    bd, rows, _ = q.shape
    t = rows // N_HEADS
    n_pool, page, _ = cache_ckv.shape
    n_pages = page_table.shape[1]
    assert n_pages % pps == 0 and t <= page and page % (16 * ROPE_PER_ROW) == 0
    sub = page // ROPE_PER_ROW
    cache_kr = cache_kr.reshape(n_pool, sub, LANE)
    cache_ckv = cache_ckv.reshape(n_pool, page * KV_LORA // LANE, LANE)
    page_spec = lambda shape, i: pl.BlockSpec((1,) + shape, lambda b, j, pt: (pt[b, j * pps + i], 0, 0))
    grid_spec = pltpu.PrefetchScalarGridSpec(
        num_scalar_prefetch=1,
        grid=(bd, n_pages // pps),
        in_specs=[pl.BlockSpec((1, rows, QK_WIDTH), lambda b, j, pt: (b, 0, 0)),
                  pl.BlockSpec((1, t, KV_LORA), lambda b, j, pt: (b, 0, 0)),
                  pl.BlockSpec((1, t, QK_ROPE), lambda b, j, pt: (b, 0, 0)),
                  pl.BlockSpec(wuv.shape, lambda b, j, pt: (0, 0, 0))]
                 + [page_spec((page * KV_LORA // LANE, LANE), i) for i in range(pps)]
                 + [page_spec((sub, LANE), i) for i in range(pps)],
        out_specs=pl.BlockSpec((1, t, ATTN_WIDTH), lambda b, j, pt: (b, 0, 0)),
        scratch_shapes=[pltpu.VMEM((rows, 1), F32), pltpu.VMEM((rows, 1), F32),
                        pltpu.VMEM((rows, KV_LORA), F32),
                        pltpu.VMEM((pps * page, KV_LORA), BF16), pltpu.VMEM((pps * page, ROPE_PAD), BF16)],
    )
    return pl.pallas_call(
        functools.partial(_paged_attn_kernel, pps=pps, page=page, t=t),
        grid_spec=grid_spec,
        out_shape=jax.ShapeDtypeStruct((bd, t, ATTN_WIDTH), F32),
        compiler_params=_params("parallel", "arbitrary"),
        name="paged_attn",
    )(page_table, q, ckv_new, kr_new, wuv, *([cache_ckv] * pps), *([cache_kr] * pps))


def _cmul(x, pw):
    return x * pw[0:1] + pltpu.roll(x, SSM_STATE, axis=1) * pw[1:2]


def _s5_kernel(*refs, nc, nb, has_h0):
    if has_h0:
        u_ref, mg_ref, bz_ref, cz_ref, pw_ref, h0_ref, y_ref, hl_ref = refs
    else:
        u_ref, mg_ref, bz_ref, cz_ref, pw_ref, y_ref, hl_ref = refs
    u = u_ref[0]
    y = _dot(u, mg_ref[0])
    s = _dot(u, bz_ref[0])
    rows = s.shape[0]
    if has_h0:
        assert nc == 1
        s_prev = h0_ref[0]
        s = s + _cmul(s_prev, pw_ref[0, 0])
    else:
        chunk = lax.broadcasted_iota(jnp.int32, (rows, STATE_LANES), 0) & (nc - 1)
        for k in range(nc.bit_length() - 1):
            sh = 1 << k
            shifted = jnp.where(chunk >= sh, pltpu.roll(s, sh, axis=0), 0.0)
            s = s + _cmul(shifted, pw_ref[0, k])
        s_prev = jnp.where(chunk >= 1, pltpu.roll(s, 1, axis=0), 0.0)
    y_ref[0] = y + _dot(s_prev.astype(BF16), cz_ref[0])
    if nc == 1:
        hl_ref[0] = s
    else:
        for b in range(nb):
            hl_ref[0, b:b + 1, :] = s[b * nc + nc - 1:b * nc + nc, :]


def _s5(u_t, mg, bz, cz, pw, h0, nc, nb):
    g, rows, kdim = u_t.shape
    blk = lambda a: pl.BlockSpec((1,) + a.shape[1:], lambda i: (i,) + (0,) * (a.ndim - 1))
    ins = [u_t, mg, bz, cz, pw] + ([h0] if h0 is not None else [])
    return pl.pallas_call(
        functools.partial(_s5_kernel, nc=nc, nb=nb, has_h0=h0 is not None),
        grid=(g,),
        in_specs=[blk(a) for a in ins],
        out_specs=[pl.BlockSpec((1, rows, kdim), lambda i: (i, 0, 0)),
                   pl.BlockSpec((1, nb, STATE_LANES), lambda i: (i, 0, 0))],
        out_shape=[jax.ShapeDtypeStruct((g, rows, kdim), F32), jax.ShapeDtypeStruct((g, nb, STATE_LANES), F32)],
        compiler_params=_params("parallel"),
        name="s5",
    )(*ins)


def _s5_weights(lam_re, lam_im, log_dt, b_re, b_im, c_re, c_im, tc, n_levels):
    hi = lax.Precision.HIGHEST
    lam = lax.complex(lam_re.astype(F32), lam_im.astype(F32))
    ldt = lam * jnp.exp(log_dt.astype(F32))[:, None]
    lam_bar = jnp.exp(ldt)
    b_bar = ((lam_bar - 1.0) / lam)[..., None] * lax.complex(b_re.astype(F32), b_im.astype(F32))
    c = lax.complex(c_re.astype(F32), c_im.astype(F32))
    steps = jnp.arange(tc + 1, dtype=F32)
    pw = jnp.exp(ldt[None] * steps[:, None, None])
    kern = jnp.real(jnp.einsum('gop,dgp,gpi->dgoi', c, pw[:tc], b_bar, precision=hi))
    lag = jnp.arange(tc)[None, :] - jnp.arange(tc)[:, None]
    mg = jnp.where((lag >= 0)[:, :, None, None, None], kern[jnp.clip(lag, 0)], 0.0)
    mg = mg.transpose(2, 0, 4, 1, 3).reshape(SSM_GROUPS, tc * SSM_GROUP, tc * SSM_GROUP)
    bzc = pw[tc - 1 - jnp.arange(tc)][..., None] * b_bar[None]
    bzc = bzc.transpose(1, 0, 3, 2).reshape(SSM_GROUPS, tc * SSM_GROUP, SSM_STATE)
    bz = jnp.concatenate([jnp.real(bzc), jnp.imag(bzc)], axis=-1)
    czc = c[None] * pw[1:tc + 1][:, :, None, :]
    czc = czc.transpose(1, 3, 0, 2).reshape(SSM_GROUPS, SSM_STATE, tc * SSM_GROUP)
    cz = jnp.concatenate([jnp.real(czc), -jnp.imag(czc)], axis=1)
    lev = jnp.exp(ldt[None] * (tc * 2.0 ** jnp.arange(n_levels, dtype=F32))[:, None, None])
    re, im = jnp.real(lev), jnp.imag(lev)
    pwl = jnp.stack([jnp.concatenate([re, re], -1), jnp.concatenate([-im, im], -1)], axis=2)
    return mg.astype(BF16), bz.astype(BF16), cz.astype(BF16), pwl.transpose(1, 0, 2, 3)


def _s5_branch(u, h0, ssm_w, batch, seq, tc):
    nc = seq // tc
    n_levels = max(nc.bit_length() - 1, 1)
    assert nc & (nc - 1) == 0 and (h0 is None or nc == 1)
    mg, bz, cz, pwl = _s5_weights(*ssm_w, tc, n_levels)
    u_t = u.astype(BF16).reshape(batch, nc, tc, SSM_GROUPS, SSM_GROUP).transpose(3, 0, 1, 2, 4)
    u_t = u_t.reshape(SSM_GROUPS, batch * nc, tc * SSM_GROUP)
    h0_t = None
    if h0 is not None:
        h0_t = jnp.concatenate([h0[..., 0], h0[..., 1]], axis=-1).astype(F32).transpose(1, 0, 2)
    y_t, hl = _s5(u_t, mg, bz, cz, pwl, h0_t, nc, batch)
    y = y_t.reshape(SSM_GROUPS, batch, nc, tc, SSM_GROUP).transpose(1, 2, 3, 0, 4).reshape(batch * seq, -1)
    hl = hl.transpose(1, 0, 2)
    state = jnp.stack([hl[..., :SSM_STATE], hl[..., SSM_STATE:]], axis=-1)
    return y, state


def _merge_kernel(x_ref, y_ref, u_ref, gs_ref, o_ref, ga_ref, ms_ref, ma_ref,
                  d_ref, wglu_ref, bglu_ref, wbs_ref, wba_ref, wout_ref, gf_ref, out_ref):
    y = y_ref[...] + d_ref[...] * u_ref[...]
    zg = jax.nn.gelu(y)
    glu = zg * jax.nn.sigmoid(_dot(zg.astype(BF16), wglu_ref[...]) + bglu_ref[...])
    y_s = _dot((glu * jax.nn.silu(gs_ref[...])).astype(BF16), wbs_ref[...])
    y_a = _dot((o_ref[...] * jax.nn.silu(ga_ref[...])).astype(BF16), wba_ref[...])
    merged = jax.nn.sigmoid(ms_ref[...]) * y_s + jax.nn.sigmoid(ma_ref[...]) * y_a
    h = x_ref[...] + _dot(merged.astype(BF16), wout_ref[...])
    out_ref[...] = _rms(h, gf_ref[...])


def _merge(x, y, u, gs, o, ga, ms, ma, d, wglu, bglu, wbs, wba, wout, gf, tm):
    n = x.shape[0]
    row = pl.BlockSpec((tm, D_MODEL), lambda i: (i, 0))
    full = lambda a: pl.BlockSpec(a.shape, lambda i: (0,) * a.ndim)
    consts = (d, wglu, bglu, wbs, wba, wout, gf)
    return pl.pallas_call(
        _merge_kernel,
        grid=(n // tm,),
        in_specs=[row] * 8 + [full(a) for a in consts],
        out_specs=row,
        out_shape=jax.ShapeDtypeStruct((n, D_MODEL), F32),
        compiler_params=_params("parallel"),
        name="merge",
    )(x, y, u, gs, o, ga, ms, ma, *consts)


def _rope_tables(pos, reps, lane_reps):
    half = QK_ROPE // 2
    inv = ROPE_BASE ** (-jnp.arange(half, dtype=F32) * (2.0 / QK_ROPE))
    ang = pos.astype(F32)[:, None] * inv[None, :]
    cos, sin = jnp.cos(ang), jnp.sin(ang)
    cos_t = _pad_last(jnp.tile(jnp.concatenate([cos, cos], axis=-1), (1, lane_reps)), ROPE_PAD)
    sin_t = _pad_last(jnp.tile(jnp.concatenate([-sin, sin], axis=-1), (1, lane_reps)), ROPE_PAD)
    return jnp.tile(cos_t, (reps, 1)), jnp.tile(sin_t, (reps, 1))


def _swap_halves(w):
    half = QK_ROPE // 2
    return jnp.concatenate([w[..., half:], w[..., :half]], axis=-1)


def _pad_last(w, width):
    return jnp.pad(w, [(0, 0)] * (w.ndim - 1) + [(0, width - w.shape[-1])])


def _rope_weight(w_rope, lane_reps):
    return _pad_last(jnp.tile(w_rope, (1, 1, lane_reps)), ROPE_PAD).reshape(Q_LORA, N_HEADS * ROPE_PAD).astype(BF16)


def _pack_weights(w_in, mla_w_uq, mla_w_uk, mla_w_uv):
    b = [0]
    for w in (D_MODEL, D_MODEL, Q_LORA, KV_LORA, QK_ROPE, ATTN_WIDTH, D_MODEL, D_MODEL):
        b.append(b[-1] + w)
    w_kr = w_in[:, b[4]:b[5]]
    w_all = jnp.concatenate([w_in[:, :b[4]], _pad_last(w_kr, ROPE_PAD), _pad_last(_swap_halves(w_kr), ROPE_PAD),
                             w_in[:, b[5]:]], axis=-1).astype(BF16)
    w_nope = _pad_last(mla_w_uq[..., :QK_NOPE], LANE).reshape(Q_LORA, N_HEADS * LANE).astype(BF16)
    w_rope = mla_w_uq[..., QK_NOPE:]
    w_uk = jnp.pad(mla_w_uk.transpose(1, 2, 0), ((0, 0), (0, LANE - QK_NOPE), (0, 0))).astype(BF16)
    w_uv = mla_w_uv.transpose(1, 0, 2)
    lo = jnp.pad(w_uv, ((0, 0), (0, 0), (0, LANE - V_DIM)))
    hi = jnp.pad(w_uv, ((0, 0), (0, 0), (LANE - V_DIM, 0)))
    w_uv_pair = jnp.where((jnp.arange(N_HEADS) % 2 == 0)[:, None, None], lo, hi).astype(BF16)
    return dict(
        w_all=w_all, w_nope=w_nope, w_uk=w_uk, w_uv_pair=w_uv_pair,
        w_r=_rope_weight(w_rope, 1), w_rs=_rope_weight(_swap_halves(w_rope), 1),
        w_r_rep=_rope_weight(w_rope, ROPE_PER_ROW), w_rs_rep=_rope_weight(_swap_halves(w_rope), ROPE_PER_ROW),
        w_uv_t=w_uv.transpose(0, 2, 1).astype(BF16))


def _layer(x, pos, h0, attend, w, batch, seq, tc, tm, lane_reps):
    n = x.shape[0]
    cos_t, sin_t = _rope_tables(pos, max(tm // seq, 1), lane_reps)
    row = lambda v: v.reshape(1, -1).astype(F32)
    u, gs, cq, ckv, kr, kf, vt, ga, ms, ma = _in_proj(x, row(w["norm_in"]), w["w_all"], row(w["mla_kv_norm"]),
                                                      cos_t, sin_t, tm)
    o = attend(cq, cos_t, sin_t, ckv, kr, kf, vt).reshape(n, ATTN_WIDTH)
    y_ssm, state = _s5_branch(u, h0, w["ssm"], batch, seq, tc)
    y = _merge(x, y_ssm, u, gs, o, ga, ms, ma, row(w["ssm_d"]), w["ssm_w_glu"].astype(BF16), row(w["ssm_b_glu"]),
               w["w_br_ssm"].astype(BF16), w["w_br_attn"].astype(BF16), w["w_out"].astype(BF16),
               row(w["norm_final"]), tm)
    return y, ckv, kr, state


def kernel(x_prompt, x_sample, cache_ckv, cache_krope, state_ssm, page_table,
           norm_in, w_in, ssm_lambda_re, ssm_lambda_im, ssm_log_dt, ssm_b_re, ssm_b_im,
           ssm_c_re, ssm_c_im, ssm_d, ssm_w_glu, ssm_b_glu, w_br_ssm,
           mla_q_norm, mla_w_uq, mla_kv_norm, mla_w_uk, mla_w_uv, w_br_attn, w_out, norm_final):
    b, l, _ = x_prompt.shape
    bd, t, _ = x_sample.shape
    w = dict(norm_in=norm_in, mla_kv_norm=mla_kv_norm,
             ssm=(ssm_lambda_re, ssm_lambda_im, ssm_log_dt, ssm_b_re, ssm_b_im, ssm_c_re, ssm_c_im),
             ssm_d=ssm_d, ssm_w_glu=ssm_w_glu, ssm_b_glu=ssm_b_glu, w_br_ssm=w_br_ssm,
             w_br_attn=w_br_attn, w_out=w_out, norm_final=norm_final,
             **_pack_weights(w_in, mla_w_uq, mla_w_uk, mla_w_uv))
    tm = 256
    gq = mla_q_norm.reshape(1, -1).astype(F32)

    def attend_prompt(cq, cos_t, sin_t, ckv, kr, kf, vt):
        qt = _q_prep_t(cq, gq, w["w_nope"].T, w["w_r"].T, w["w_rs"].T, w["w_uk"].transpose(0, 2, 1),
                       cos_t.T, sin_t.T, tm, b, l)
        return _prompt_attn(qt, kf.reshape(b, l, QK_WIDTH), vt, w["w_uv_t"], tm, cg=1024)

    def attend_sample(cq, cos_t, sin_t, ckv, kr, kf, vt):
        q = _q_prep(cq, gq, w["w_nope"], w["w_r_rep"], w["w_rs_rep"], w["w_uk"], cos_t, sin_t, tm, bd, t, F32)
        return _paged_attn(q.reshape(bd, N_HEADS * t, QK_WIDTH), ckv.reshape(bd, t, KV_LORA),
                           kr.reshape(bd, t, QK_ROPE), cache_ckv, cache_krope, page_table, w["w_uv_pair"], pps=16)

    y_p, ckv_p, kr_p, ssm_p = _layer(x_prompt.reshape(b * l, D_MODEL), jnp.arange(l), None, attend_prompt,
                                     w, b, l, 16, tm, 1)
    y_s, ckv_s, kr_s, ssm_s = _layer(x_sample.reshape(bd * t, D_MODEL), PAST_LEN + jnp.arange(t), state_ssm,
                                     attend_sample, w, bd, t, t, tm, ROPE_PER_ROW)
    return (y_p.reshape(b, l, D_MODEL), y_s.reshape(bd, t, D_MODEL),
            ckv_p.reshape(b, l, KV_LORA), kr_p.reshape(b, l, QK_ROPE), ssm_p,
            ckv_s.reshape(bd, t, KV_LORA), kr_s.reshape(bd, t, QK_ROPE), ssm_s)
```

```python
import functools
import math

import jax
import jax.numpy as jnp
from jax import lax
from jax.experimental import pallas as pl
from jax.experimental.pallas import tpu as pltpu

D_MODEL = 1024
PAST_LEN = 16384
SSM_GROUP = 16
SSM_GROUPS = 64
SSM_STATE = 64
N_HEADS = 16
QK_NOPE = 64
QK_ROPE = 32
V_DIM = 64
Q_LORA = 384
KV_LORA = 256
ATTN_WIDTH = N_HEADS * V_DIM
ROPE_BASE = 10000.0
ATTN_SCALE = (QK_NOPE + QK_ROPE) ** -0.5
NEG_INF = -1e30
NORM_EPS = 1e-6

LANE = 128
ROPE_PAD = LANE
QK_WIDTH = KV_LORA + ROPE_PAD
STATE_LANES = 2 * SSM_STATE
VMEM_LIMIT = 56 * 1024 * 1024

F32 = jnp.float32
BF16 = jnp.bfloat16

_C_U = 0
_C_GS = _C_U + D_MODEL
_C_CQ = _C_GS + D_MODEL
_C_CKV = _C_CQ + Q_LORA
_C_KR = _C_CKV + KV_LORA
_C_KRS = _C_KR + ROPE_PAD
_C_GA = _C_KRS + ROPE_PAD
_C_MS = _C_GA + ATTN_WIDTH
_C_MA = _C_MS + D_MODEL
_C_END = _C_MA + D_MODEL


def _rms(x, g):
    return x * lax.rsqrt(jnp.mean(x * x, axis=-1, keepdims=True) + NORM_EPS) * g


def _dot(a, b):
    return jnp.dot(a, b, preferred_element_type=F32)


def _dot_nt(a, b):
    return lax.dot_general(a, b, (((1,), (1,)), ((), ())), preferred_element_type=F32)


def _params(*sem):
    return pltpu.CompilerParams(dimension_semantics=sem, vmem_limit_bytes=VMEM_LIMIT)


def _in_proj_kernel(x_ref, gin_ref, w_ref, gkv_ref, cos_ref, sin_ref,
                    u_ref, gs_ref, cq_ref, ckv_ref, kr_ref, kf_ref, vt_ref, ga_ref, ms_ref, ma_ref):
    xn = _rms(x_ref[...], gin_ref[...]).astype(BF16)

    def proj(lo, hi):
        return _dot(xn, w_ref[:, lo:hi])

    u_ref[...] = proj(_C_U, _C_GS)
    gs_ref[...] = proj(_C_GS, _C_CQ)
    cq_ref[...] = proj(_C_CQ, _C_CKV)
    ckv = _rms(proj(_C_CKV, _C_KR), gkv_ref[...])
    ckv_ref[...] = ckv
    kr = proj(_C_KR, _C_KRS) * cos_ref[...] + proj(_C_KRS, _C_GA) * sin_ref[...]
    kr_ref[...] = kr[:, :QK_ROPE]
    kf_ref[:, :KV_LORA] = ckv.astype(BF16)
    kf_ref[:, KV_LORA:] = kr.astype(BF16)
    vt_ref[0] = ckv.T.astype(BF16)
    ga_ref[...] = proj(_C_GA, _C_MS)
    ms_ref[...] = proj(_C_MS, _C_MA)
    ma_ref[...] = proj(_C_MA, _C_END)


def _in_proj(x, gin, w_all, gkv, cos_k, sin_k, tm):
    n = x.shape[0]
    nblk = cos_k.shape[0] // tm
    row = lambda w: pl.BlockSpec((tm, w), lambda i: (i, 0))
    full = lambda a: pl.BlockSpec(a.shape, lambda i: (0,) * a.ndim)
    tab = pl.BlockSpec((tm, ROPE_PAD), lambda i: (i % nblk, 0))
    widths = (D_MODEL, D_MODEL, Q_LORA, KV_LORA, QK_ROPE, QK_WIDTH, None, ATTN_WIDTH, D_MODEL, D_MODEL)
    dtypes = (F32, F32, F32, F32, F32, BF16, BF16, F32, F32, F32)
    vt_spec = pl.BlockSpec((1, KV_LORA, tm), lambda i: (i, 0, 0))
    shape = lambda w: (n, w) if w else (n // tm, KV_LORA, tm)
    return pl.pallas_call(
        _in_proj_kernel,
        grid=(n // tm,),
        in_specs=[row(D_MODEL), full(gin), full(w_all), full(gkv), tab, tab],
        out_specs=[row(w) if w else vt_spec for w in widths],
        out_shape=[jax.ShapeDtypeStruct(shape(w), d) for w, d in zip(widths, dtypes)],
        compiler_params=_params("parallel"),
        name="in_proj",
    )(x, gin, w_all, gkv, cos_k, sin_k)


def _q_prep_kernel(cq_ref, gq_ref, wn_ref, wr_ref, wrs_ref, wuk_ref, cos_ref, sin_ref, q_ref):
    nb, _, t, _ = q_ref.shape
    cqn = _rms(cq_ref[...], gq_ref[...]).astype(BF16)
    qn = _dot(cqn, wn_ref[...]).astype(BF16)
    qr = _dot(cqn, wr_ref[...])
    qrs = _dot(cqn, wrs_ref[...])
    cos = cos_ref[...]
    sin = sin_ref[...]
    for h in range(N_HEADS):
        sl = slice(h * LANE, (h + 1) * LANE)
        q_lat = _dot(qn[:, sl], wuk_ref[h])
        q_rope = qr[:, sl] * cos + qrs[:, sl] * sin
        q_ref[:, h, :, :KV_LORA] = q_lat.reshape(nb, t, KV_LORA).astype(q_ref.dtype)
        q_ref[:, h, :, KV_LORA:] = q_rope.reshape(nb, t, ROPE_PAD).astype(q_ref.dtype)


def _q_prep(cq, gq, wn, wr, wrs, wuk, cos_q, sin_q, tm, batch, seq, out_dtype):
    n = cq.shape[0]
    nblk = cos_q.shape[0] // tm
    full = lambda a: pl.BlockSpec(a.shape, lambda i: (0,) * a.ndim)
    tab = pl.BlockSpec((tm, ROPE_PAD), lambda i: (i % nblk, 0))
    if seq >= tm:
        per = seq // tm
        out_spec = pl.BlockSpec((1, N_HEADS, tm, QK_WIDTH), lambda i: (i // per, 0, i % per, 0))
    else:
        out_spec = pl.BlockSpec((tm // seq, N_HEADS, seq, QK_WIDTH), lambda i: (i, 0, 0, 0))
    return pl.pallas_call(
        _q_prep_kernel,
        grid=(n // tm,),
        in_specs=[pl.BlockSpec((tm, Q_LORA), lambda i: (i, 0)), full(gq), full(wn), full(wr), full(wrs),
                  full(wuk), tab, tab],
        out_specs=out_spec,
        out_shape=jax.ShapeDtypeStruct((batch, N_HEADS, seq, QK_WIDTH), out_dtype),
        compiler_params=_params("parallel"),
        name="q_prep",
    )(cq, gq, wn, wr, wrs, wuk, cos_q, sin_q)


def _q_prep_t_kernel(cq_ref, gq_ref, wn_ref, wr_ref, wrs_ref, wuk_ref, cos_ref, sin_ref, q_ref):
    tm = cq_ref.shape[0]
    cqn = _rms(cq_ref[...], gq_ref[...]).astype(BF16)
    qn = _dot_nt(wn_ref[...], cqn).astype(BF16)
    qr = _dot_nt(wr_ref[...], cqn)
    qrs = _dot_nt(wrs_ref[...], cqn)
    cos = cos_ref[...]
    sin = sin_ref[...]
    for h in range(N_HEADS):
        sl = slice(h * LANE, (h + 1) * LANE)
        cl = slice(h * tm, (h + 1) * tm)
        q_ref[0, 0, :KV_LORA, cl] = _dot(wuk_ref[h], qn[sl]).astype(BF16)
        q_ref[0, 0, KV_LORA:, cl] = (qr[sl] * cos + qrs[sl] * sin).astype(BF16)


def _q_prep_t(cq, gq, wn_t, wr_t, wrs_t, wuk_t, cos_qt, sin_qt, tm, batch, seq):
    n = cq.shape[0]
    per = seq // tm
    full = lambda a: pl.BlockSpec(a.shape, lambda i: (0,) * a.ndim)
    tab = pl.BlockSpec((ROPE_PAD, tm), lambda i: (0, i % per))
    return pl.pallas_call(
        _q_prep_t_kernel,
        grid=(n // tm,),
        in_specs=[pl.BlockSpec((tm, Q_LORA), lambda i: (i, 0)), full(gq), full(wn_t), full(wr_t), full(wrs_t),
                  full(wuk_t), tab, tab],
        out_specs=pl.BlockSpec((1, 1, QK_WIDTH, N_HEADS * tm), lambda i: (i // per, i % per, 0, 0)),
        out_shape=jax.ShapeDtypeStruct((batch, per, QK_WIDTH, N_HEADS * tm), BF16),
        compiler_params=_params("parallel"),
        name="q_prep_t",
    )(cq, gq, wn_t, wr_t, wrs_t, wuk_t, cos_qt, sin_qt)


def _softmax_step(s, v, m_ref, l_ref, acc_ref):
    m_prev = m_ref[...]
    m_new = jnp.maximum(m_prev, jnp.max(s, axis=-1, keepdims=True))
    alpha = jnp.exp(m_prev - m_new)
    p = jnp.exp(s - m_new)
    l_ref[...] = alpha * l_ref[...] + jnp.sum(p, axis=-1, keepdims=True)
    acc_ref[...] = alpha * acc_ref[...] + _dot(p.astype(BF16), v)
    m_ref[...] = m_new


def _value_up(o_lat, wuv_ref, t):
    cols = []
    for j in range(N_HEADS // 2):
        a = o_lat[(2 * j) * t:(2 * j + 1) * t].astype(BF16)
        b = o_lat[(2 * j + 1) * t:(2 * j + 2) * t].astype(BF16)
        cols.append(_dot(a, wuv_ref[2 * j]) + _dot(b, wuv_ref[2 * j + 1]))
    return jnp.concatenate(cols, axis=-1)


def _init_softmax(m_ref, l_ref, acc_ref):
    m_ref[...] = jnp.full(m_ref.shape, -jnp.inf, F32)
    l_ref[...] = jnp.zeros(l_ref.shape, F32)
    acc_ref[...] = jnp.zeros(acc_ref.shape, F32)


def _prompt_attn_kernel(qt_ref, k_ref, vt_ref, wuvt_ref, o_ref, m_ref, l_ref, acc_ref, *, tq, cg):
    qi = pl.program_id(1)
    cols = N_HEADS * tq
    _init_softmax(m_ref, l_ref, acc_ref)

    def chunk(kb, diagonal):
        k = k_ref[0, pl.ds(pl.multiple_of(kb * tq, tq), tq), :]
        vt = vt_ref[kb]
        if diagonal:
            key = lax.broadcasted_iota(jnp.int32, (tq, cg), 0)
            tok = lax.broadcasted_iota(jnp.int32, (tq, cg), 1) & (tq - 1)
            causal = key <= tok
        for g in range(cols // cg):
            sl = slice(g * cg, (g + 1) * cg)
            s = _dot(k, qt_ref[0, 0, :, sl]) * ATTN_SCALE
            if diagonal:
                s = jnp.where(causal, s, NEG_INF)
            m_prev = m_ref[:, sl]
            m_new = jnp.maximum(m_prev, jnp.max(s, axis=0, keepdims=True))
            alpha = jnp.exp(m_prev - m_new)
            p = jnp.exp(s - m_new)
            l_ref[:, sl] = alpha * l_ref[:, sl] + jnp.sum(p, axis=0, keepdims=True)
            acc_ref[:, sl] = alpha * acc_ref[:, sl] + _dot(vt, p.astype(BF16))
            m_ref[:, sl] = m_new

    def body(kb, carry):
        chunk(kb, False)
        return carry

    lax.fori_loop(0, qi, body, 0)
    chunk(qi, True)
    heads = []
    for h in range(N_HEADS):
        sl = slice(h * tq, (h + 1) * tq)
        o_lat = (acc_ref[:, sl] / l_ref[:, sl]).astype(BF16)
        heads.append(_dot(wuvt_ref[h], o_lat))
    o_ref[0] = jnp.concatenate(heads, axis=0).T


def _prompt_attn(qt, kf, vt, wuv_t, tq, cg):
    b, nq, _, cols = qt.shape
    l = nq * tq
    return pl.pallas_call(
        functools.partial(_prompt_attn_kernel, tq=tq, cg=cg),
        grid=(b, nq),
        in_specs=[pl.BlockSpec((1, 1, QK_WIDTH, cols), lambda bi, qi: (bi, qi, 0, 0)),
                  pl.BlockSpec((1, l, QK_WIDTH), lambda bi, qi: (bi, 0, 0)),
                  pl.BlockSpec((nq, KV_LORA, tq), lambda bi, qi: (bi, 0, 0)),
                  pl.BlockSpec(wuv_t.shape, lambda bi, qi: (0, 0, 0))],
        out_specs=pl.BlockSpec((1, tq, ATTN_WIDTH), lambda bi, qi: (bi, qi, 0)),
        out_shape=jax.ShapeDtypeStruct((b, l, ATTN_WIDTH), F32),
        scratch_shapes=[pltpu.VMEM((1, cols), F32), pltpu.VMEM((1, cols), F32),
                        pltpu.VMEM((KV_LORA, cols), F32)],
        compiler_params=_params("parallel", "arbitrary"),
        name="prompt_attn",
    )(qt, kf, vt, wuv_t)


def _paged_attn_kernel(pt_ref, q_ref, ks_ref, rs_ref, wuv_ref, ckv_hbm, kr_hbm, o_ref,
                       m_ref, l_ref, acc_ref, kbuf, rbuf, sem, *, pps, page, t):
    b = pl.program_id(0)
    j = pl.program_id(1)
    nj = pl.num_programs(1)
    step = b * nj + j
    slot = step % 2
    rows = N_HEADS * t

    def page_copies(bb, jj, sl):
        for i in range(pps):
            pg = pt_ref[bb, jj * pps + i]
            yield pltpu.make_async_copy(ckv_hbm.at[pg], kbuf.at[sl, i], sem.at[0, sl])
            yield pltpu.make_async_copy(kr_hbm.at[pg], rbuf.at[sl, i], sem.at[1, sl])

    @pl.when(step == 0)
    def _():
        for cp in page_copies(b, j, slot):
            cp.start()

    @pl.when(step + 1 < pl.num_programs(0) * nj)
    def _():
        last = j == nj - 1
        for cp in page_copies(jnp.where(last, b + 1, b), jnp.where(last, 0, j + 1), 1 - slot):
            cp.start()

    q = q_ref[0].astype(BF16)
    q_lat = q[:, :KV_LORA]
    q_rope = q[:, KV_LORA:KV_LORA + QK_ROPE]

    def scores(k, r):
        return (_dot_nt(q_lat, k) + _dot_nt(q_rope, r)) * ATTN_SCALE

    @pl.when(j == 0)
    def _():
        _init_softmax(m_ref, l_ref, acc_ref)
        k = jnp.concatenate([ks_ref[0], jnp.zeros((page - t, KV_LORA), F32)], axis=0).astype(BF16)
        r = jnp.concatenate([rs_ref[0], jnp.zeros((page - t, QK_ROPE), F32)], axis=0).astype(BF16)
        s = scores(k, r)
        tok = lax.broadcasted_iota(jnp.int32, (t, page), 0)
        key = lax.broadcasted_iota(jnp.int32, (t, page), 1)
        s = jnp.where((key <= tok)[None], s.reshape(N_HEADS, t, page), NEG_INF).reshape(rows, page)
        _softmax_step(s, k, m_ref, l_ref, acc_ref)

    for cp in page_copies(b, j, slot):
        cp.wait()
    k = kbuf[slot].reshape(pps * page, KV_LORA).astype(BF16)
    r = rbuf[slot].reshape(pps * page, QK_ROPE).astype(BF16)
    _softmax_step(scores(k, r), k, m_ref, l_ref, acc_ref)

    @pl.when(j == nj - 1)
    def _():
        o_ref[0] = _value_up(acc_ref[...] / l_ref[...], wuv_ref, t)


def _paged_attn(q, ckv_new, kr_new, cache_ckv, cache_kr, page_table, wuv, pps):
    bd, rows, _ = q.shape
    t = rows // N_HEADS
    _, page, _ = cache_ckv.shape
    n_pages = page_table.shape[1]
    assert n_pages % pps == 0 and t <= page
    grid_spec = pltpu.PrefetchScalarGridSpec(
        num_scalar_prefetch=1,
        grid=(bd, n_pages // pps),
        in_specs=[pl.BlockSpec((1, rows, QK_WIDTH), lambda b, j, pt: (b, 0, 0)),
                  pl.BlockSpec((1, t, KV_LORA), lambda b, j, pt: (b, 0, 0)),
                  pl.BlockSpec((1, t, QK_ROPE), lambda b, j, pt: (b, 0, 0)),
                  pl.BlockSpec(wuv.shape, lambda b, j, pt: (0, 0, 0)),
                  pl.BlockSpec(memory_space=pl.ANY), pl.BlockSpec(memory_space=pl.ANY)],
        out_specs=pl.BlockSpec((1, t, ATTN_WIDTH), lambda b, j, pt: (b, 0, 0)),
        scratch_shapes=[pltpu.VMEM((rows, 1), F32), pltpu.VMEM((rows, 1), F32),
                        pltpu.VMEM((rows, KV_LORA), F32),
                        pltpu.VMEM((2, pps, page, KV_LORA), F32), pltpu.VMEM((2, pps, page, QK_ROPE), F32),
                        pltpu.SemaphoreType.DMA((2, 2))],
    )
    return pl.pallas_call(
        functools.partial(_paged_attn_kernel, pps=pps, page=page, t=t),
        grid_spec=grid_spec,
        out_shape=jax.ShapeDtypeStruct((bd, t, ATTN_WIDTH), F32),
        compiler_params=_params("arbitrary", "arbitrary"),
        name="paged_attn",
    )(page_table, q, ckv_new, kr_new, wuv, cache_ckv, cache_kr)


def _cmul(x, pw):
    return x * pw[0:1] + pltpu.roll(x, SSM_STATE, axis=1) * pw[1:2]


def _s5_kernel(*refs, nc, nb, has_h0):
    if has_h0:
        u_ref, mg_ref, bz_ref, cz_ref, pw_ref, h0_ref, y_ref, hl_ref = refs
    else:
        u_ref, mg_ref, bz_ref, cz_ref, pw_ref, y_ref, hl_ref = refs
    u = u_ref[0]
    y = _dot(u, mg_ref[0])
    s = _dot(u, bz_ref[0])
    rows = s.shape[0]
    if has_h0:
        assert nc == 1
        s_prev = h0_ref[0]
        s = s + _cmul(s_prev, pw_ref[0, 0])
    else:
        chunk = lax.broadcasted_iota(jnp.int32, (rows, STATE_LANES), 0) & (nc - 1)
        for k in range(nc.bit_length() - 1):
            sh = 1 << k
            shifted = jnp.where(chunk >= sh, pltpu.roll(s, sh, axis=0), 0.0)
            s = s + _cmul(shifted, pw_ref[0, k])
        s_prev = jnp.where(chunk >= 1, pltpu.roll(s, 1, axis=0), 0.0)
    y_ref[0] = y + _dot(s_prev.astype(BF16), cz_ref[0])
    if nc == 1:
        hl_ref[0] = s
    else:
        for b in range(nb):
            hl_ref[0, b:b + 1, :] = s[b * nc + nc - 1:b * nc + nc, :]


def _s5(u_t, mg, bz, cz, pw, h0, nc, nb):
    g, rows, kdim = u_t.shape
    blk = lambda a: pl.BlockSpec((1,) + a.shape[1:], lambda i: (i,) + (0,) * (a.ndim - 1))
    ins = [u_t, mg, bz, cz, pw] + ([h0] if h0 is not None else [])
    return pl.pallas_call(
        functools.partial(_s5_kernel, nc=nc, nb=nb, has_h0=h0 is not None),
        grid=(g,),
        in_specs=[blk(a) for a in ins],
        out_specs=[pl.BlockSpec((1, rows, kdim), lambda i: (i, 0, 0)),
                   pl.BlockSpec((1, nb, STATE_LANES), lambda i: (i, 0, 0))],
        out_shape=[jax.ShapeDtypeStruct((g, rows, kdim), F32), jax.ShapeDtypeStruct((g, nb, STATE_LANES), F32)],
        compiler_params=_params("parallel"),
        name="s5",
    )(*ins)


def _s5_weights(lam_re, lam_im, log_dt, b_re, b_im, c_re, c_im, tc, n_levels):
    hi = lax.Precision.HIGHEST
    lam = lax.complex(lam_re.astype(F32), lam_im.astype(F32))
    ldt = lam * jnp.exp(log_dt.astype(F32))[:, None]
    lam_bar = jnp.exp(ldt)
    b_bar = ((lam_bar - 1.0) / lam)[..., None] * lax.complex(b_re.astype(F32), b_im.astype(F32))
    c = lax.complex(c_re.astype(F32), c_im.astype(F32))
    steps = jnp.arange(tc + 1, dtype=F32)
    pw = jnp.exp(ldt[None] * steps[:, None, None])
    kern = jnp.real(jnp.einsum('gop,dgp,gpi->dgoi', c, pw[:tc], b_bar, precision=hi))
    lag = jnp.arange(tc)[None, :] - jnp.arange(tc)[:, None]
    mg = jnp.where((lag >= 0)[:, :, None, None, None], kern[jnp.clip(lag, 0)], 0.0)
    mg = mg.transpose(2, 0, 4, 1, 3).reshape(SSM_GROUPS, tc * SSM_GROUP, tc * SSM_GROUP)
    bzc = pw[tc - 1 - jnp.arange(tc)][..., None] * b_bar[None]
    bzc = bzc.transpose(1, 0, 3, 2).reshape(SSM_GROUPS, tc * SSM_GROUP, SSM_STATE)
    bz = jnp.concatenate([jnp.real(bzc), jnp.imag(bzc)], axis=-1)
    czc = c[None] * pw[1:tc + 1][:, :, None, :]
    czc = czc.transpose(1, 3, 0, 2).reshape(SSM_GROUPS, SSM_STATE, tc * SSM_GROUP)
    cz = jnp.concatenate([jnp.real(czc), -jnp.imag(czc)], axis=1)
    lev = jnp.exp(ldt[None] * (tc * 2.0 ** jnp.arange(n_levels, dtype=F32))[:, None, None])
    re, im = jnp.real(lev), jnp.imag(lev)
    pwl = jnp.stack([jnp.concatenate([re, re], -1), jnp.concatenate([-im, im], -1)], axis=2)
    return mg.astype(BF16), bz.astype(BF16), cz.astype(BF16), pwl.transpose(1, 0, 2, 3)


def _s5_branch(u, h0, ssm_w, batch, seq, tc):
    nc = seq // tc
    n_levels = max(nc.bit_length() - 1, 1)
    assert nc & (nc - 1) == 0 and (h0 is None or nc == 1)
    mg, bz, cz, pwl = _s5_weights(*ssm_w, tc, n_levels)
    u_t = u.astype(BF16).reshape(batch, nc, tc, SSM_GROUPS, SSM_GROUP).transpose(3, 0, 1, 2, 4)
    u_t = u_t.reshape(SSM_GROUPS, batch * nc, tc * SSM_GROUP)
    h0_t = None
    if h0 is not None:
        h0_t = jnp.concatenate([h0[..., 0], h0[..., 1]], axis=-1).astype(F32).transpose(1, 0, 2)
    y_t, hl = _s5(u_t, mg, bz, cz, pwl, h0_t, nc, batch)
    y = y_t.reshape(SSM_GROUPS, batch, nc, tc, SSM_GROUP).transpose(1, 2, 3, 0, 4).reshape(batch * seq, -1)
    hl = hl.transpose(1, 0, 2)
    state = jnp.stack([hl[..., :SSM_STATE], hl[..., SSM_STATE:]], axis=-1)
    return y, state


def _merge_kernel(x_ref, y_ref, u_ref, gs_ref, o_ref, ga_ref, ms_ref, ma_ref,
                  d_ref, wglu_ref, bglu_ref, wbs_ref, wba_ref, wout_ref, gf_ref, out_ref):
    y = y_ref[...] + d_ref[...] * u_ref[...]
    zg = jax.nn.gelu(y)
    glu = zg * jax.nn.sigmoid(_dot(zg.astype(BF16), wglu_ref[...]) + bglu_ref[...])
    y_s = _dot((glu * jax.nn.silu(gs_ref[...])).astype(BF16), wbs_ref[...])
    y_a = _dot((o_ref[...] * jax.nn.silu(ga_ref[...])).astype(BF16), wba_ref[...])
    merged = jax.nn.sigmoid(ms_ref[...]) * y_s + jax.nn.sigmoid(ma_ref[...]) * y_a
    h = x_ref[...] + _dot(merged.astype(BF16), wout_ref[...])
    out_ref[...] = _rms(h, gf_ref[...])


def _merge(x, y, u, gs, o, ga, ms, ma, d, wglu, bglu, wbs, wba, wout, gf, tm):
    n = x.shape[0]
    row = pl.BlockSpec((tm, D_MODEL), lambda i: (i, 0))
    full = lambda a: pl.BlockSpec(a.shape, lambda i: (0,) * a.ndim)
    consts = (d, wglu, bglu, wbs, wba, wout, gf)
    return pl.pallas_call(
        _merge_kernel,
        grid=(n // tm,),
        in_specs=[row] * 8 + [full(a) for a in consts],
        out_specs=row,
        out_shape=jax.ShapeDtypeStruct((n, D_MODEL), F32),
        compiler_params=_params("parallel"),
        name="merge",
    )(x, y, u, gs, o, ga, ms, ma, *consts)


def _rope_tables(pos, reps):
    half = QK_ROPE // 2
    inv = ROPE_BASE ** (-jnp.arange(half, dtype=F32) * (2.0 / QK_ROPE))
    ang = pos.astype(F32)[:, None] * inv[None, :]
    cos, sin = jnp.cos(ang), jnp.sin(ang)
    cos_t = _pad_last(jnp.concatenate([cos, cos], axis=-1), ROPE_PAD)
    sin_t = _pad_last(jnp.concatenate([-sin, sin], axis=-1), ROPE_PAD)
    return jnp.tile(cos_t, (reps, 1)), jnp.tile(sin_t, (reps, 1))


def _swap_halves(w):
    half = QK_ROPE // 2
    return jnp.concatenate([w[..., half:], w[..., :half]], axis=-1)


def _pad_last(w, width):
    return jnp.pad(w, [(0, 0)] * (w.ndim - 1) + [(0, width - w.shape[-1])])


def _rope_weight(w_rope):
    return _pad_last(w_rope, ROPE_PAD).reshape(Q_LORA, N_HEADS * ROPE_PAD).astype(BF16)


def _pack_weights(w_in, mla_w_uq, mla_w_uk, mla_w_uv):
    b = [0]
    for w in (D_MODEL, D_MODEL, Q_LORA, KV_LORA, QK_ROPE, ATTN_WIDTH, D_MODEL, D_MODEL):
        b.append(b[-1] + w)
    w_kr = w_in[:, b[4]:b[5]]
    w_all = jnp.concatenate([w_in[:, :b[4]], _pad_last(w_kr, ROPE_PAD), _pad_last(_swap_halves(w_kr), ROPE_PAD),
                             w_in[:, b[5]:]], axis=-1).astype(BF16)
    w_nope = _pad_last(mla_w_uq[..., :QK_NOPE], LANE).reshape(Q_LORA, N_HEADS * LANE).astype(BF16)
    w_rope = mla_w_uq[..., QK_NOPE:]
    w_uk = jnp.pad(mla_w_uk.transpose(1, 2, 0), ((0, 0), (0, LANE - QK_NOPE), (0, 0))).astype(BF16)
    w_uv = mla_w_uv.transpose(1, 0, 2)
    lo = jnp.pad(w_uv, ((0, 0), (0, 0), (0, LANE - V_DIM)))
    hi = jnp.pad(w_uv, ((0, 0), (0, 0), (LANE - V_DIM, 0)))
    w_uv_pair = jnp.where((jnp.arange(N_HEADS) % 2 == 0)[:, None, None], lo, hi).astype(BF16)
    return dict(
        w_all=w_all, w_nope=w_nope, w_uk=w_uk, w_uv_pair=w_uv_pair,
        w_r=_rope_weight(w_rope), w_rs=_rope_weight(_swap_halves(w_rope)),
        w_uv_t=w_uv.transpose(0, 2, 1).astype(BF16))


def _layer(x, pos, h0, attend, w, batch, seq, tc, tm):
    n = x.shape[0]
    cos_t, sin_t = _rope_tables(pos, max(tm // seq, 1))
    row = lambda v: v.reshape(1, -1).astype(F32)
    u, gs, cq, ckv, kr, kf, vt, ga, ms, ma = _in_proj(x, row(w["norm_in"]), w["w_all"], row(w["mla_kv_norm"]),
                                                      cos_t, sin_t, tm)
    o = attend(cq, cos_t, sin_t, ckv, kr, kf, vt).reshape(n, ATTN_WIDTH)
    y_ssm, state = _s5_branch(u, h0, w["ssm"], batch, seq, tc)
    y = _merge(x, y_ssm, u, gs, o, ga, ms, ma, row(w["ssm_d"]), w["ssm_w_glu"].astype(BF16), row(w["ssm_b_glu"]),
               w["w_br_ssm"].astype(BF16), w["w_br_attn"].astype(BF16), w["w_out"].astype(BF16),
               row(w["norm_final"]), tm)
    return y, ckv, kr, state


def kernel(x_prompt, x_sample, cache_ckv, cache_krope, state_ssm, page_table,
           norm_in, w_in, ssm_lambda_re, ssm_lambda_im, ssm_log_dt, ssm_b_re, ssm_b_im,
           ssm_c_re, ssm_c_im, ssm_d, ssm_w_glu, ssm_b_glu, w_br_ssm,
           mla_q_norm, mla_w_uq, mla_kv_norm, mla_w_uk, mla_w_uv, w_br_attn, w_out, norm_final):
    b, l, _ = x_prompt.shape
    bd, t, _ = x_sample.shape
    w = dict(norm_in=norm_in, mla_kv_norm=mla_kv_norm,
             ssm=(ssm_lambda_re, ssm_lambda_im, ssm_log_dt, ssm_b_re, ssm_b_im, ssm_c_re, ssm_c_im),
             ssm_d=ssm_d, ssm_w_glu=ssm_w_glu, ssm_b_glu=ssm_b_glu, w_br_ssm=w_br_ssm,
             w_br_attn=w_br_attn, w_out=w_out, norm_final=norm_final,
             **_pack_weights(w_in, mla_w_uq, mla_w_uk, mla_w_uv))
    tm = 256
    gq = mla_q_norm.reshape(1, -1).astype(F32)

    def attend_prompt(cq, cos_t, sin_t, ckv, kr, kf, vt):
        qt = _q_prep_t(cq, gq, w["w_nope"].T, w["w_r"].T, w["w_rs"].T, w["w_uk"].transpose(0, 2, 1),
                       cos_t.T, sin_t.T, tm, b, l)
        return _prompt_attn(qt, kf.reshape(b, l, QK_WIDTH), vt, w["w_uv_t"], tm, cg=1024)

    def attend_sample(cq, cos_t, sin_t, ckv, kr, kf, vt):
        q = _q_prep(cq, gq, w["w_nope"], w["w_r"], w["w_rs"], w["w_uk"], cos_t, sin_t, tm, bd, t, F32)
        return _paged_attn(q.reshape(bd, N_HEADS * t, QK_WIDTH), ckv.reshape(bd, t, KV_LORA),
                           kr.reshape(bd, t, QK_ROPE), cache_ckv, cache_krope, page_table, w["w_uv_pair"],
                           pps=32)

    y_p, ckv_p, kr_p, ssm_p = _layer(x_prompt.reshape(b * l, D_MODEL), jnp.arange(l), None, attend_prompt,
                                     w, b, l, 16, tm)
    y_s, ckv_s, kr_s, ssm_s = _layer(x_sample.reshape(bd * t, D_MODEL), PAST_LEN + jnp.arange(t), state_ssm,
                                     attend_sample, w, bd, t, t, tm)
    return (y_p.reshape(b, l, D_MODEL), y_s.reshape(bd, t, D_MODEL),
            ckv_p.reshape(b, l, KV_LORA), kr_p.reshape(b, l, QK_ROPE), ssm_p,
            ckv_s.reshape(bd, t, KV_LORA), kr_s.reshape(bd, t, QK_ROPE), ssm_s)
```

```python
import functools
import math

import jax
import jax.numpy as jnp
from jax import lax
from jax.experimental import pallas as pl
from jax.experimental.pallas import tpu as pltpu

D_MODEL = 1024
PAST_LEN = 16384
SSM_GROUP = 16
SSM_GROUPS = 64
SSM_STATE = 64
N_HEADS = 16
QK_NOPE = 64
QK_ROPE = 32
V_DIM = 64
Q_LORA = 384
KV_LORA = 256
ATTN_WIDTH = N_HEADS * V_DIM
ROPE_BASE = 10000.0
ATTN_SCALE = (QK_NOPE + QK_ROPE) ** -0.5
NEG_INF = -1e30
NORM_EPS = 1e-6

LANE = 128
ROPE_PAD = LANE
QK_WIDTH = KV_LORA + ROPE_PAD
VMEM_LIMIT = 56 * 1024 * 1024

F32 = jnp.float32
BF16 = jnp.bfloat16

_C_U = 0
_C_GS = _C_U + D_MODEL
_C_CQ = _C_GS + D_MODEL
_C_CKV = _C_CQ + Q_LORA
_C_KR = _C_CKV + KV_LORA
_C_KRS = _C_KR + ROPE_PAD
_C_GA = _C_KRS + ROPE_PAD
_C_MS = _C_GA + ATTN_WIDTH
_C_MA = _C_MS + D_MODEL
_C_END = _C_MA + D_MODEL


def _rms(x, g):
    return x * lax.rsqrt(jnp.mean(x * x, axis=-1, keepdims=True) + NORM_EPS) * g


def _dot(a, b):
    return jnp.dot(a, b, preferred_element_type=F32)


def _dot_nt(a, b):
    return lax.dot_general(a, b, (((1,), (1,)), ((), ())), preferred_element_type=F32)


def _params(*sem):
    return pltpu.CompilerParams(dimension_semantics=sem, vmem_limit_bytes=VMEM_LIMIT)


def _in_proj_kernel(x_ref, gin_ref, w_ref, gkv_ref, cos_ref, sin_ref,
                    u_ref, gs_ref, cq_ref, ckv_ref, kr_ref, kf_ref, vt_ref, ga_ref, ms_ref, ma_ref):
    xn = _rms(x_ref[...], gin_ref[...]).astype(BF16)

    def proj(lo, hi):
        return _dot(xn, w_ref[:, lo:hi])

    u_ref[...] = proj(_C_U, _C_GS)
    gs_ref[...] = proj(_C_GS, _C_CQ)
    cq_ref[...] = proj(_C_CQ, _C_CKV)
    ckv = _rms(proj(_C_CKV, _C_KR), gkv_ref[...])
    ckv_ref[...] = ckv
    kr = proj(_C_KR, _C_KRS) * cos_ref[...] + proj(_C_KRS, _C_GA) * sin_ref[...]
    kr_ref[...] = kr[:, :QK_ROPE]
    kf_ref[:, :KV_LORA] = ckv.astype(BF16)
    kf_ref[:, KV_LORA:] = kr.astype(BF16)
    vt_ref[0] = ckv.T.astype(BF16)
    ga_ref[...] = proj(_C_GA, _C_MS)
    ms_ref[...] = proj(_C_MS, _C_MA)
    ma_ref[...] = proj(_C_MA, _C_END)


def _in_proj(x, gin, w_all, gkv, cos_k, sin_k, tm):
    n = x.shape[0]
    nblk = cos_k.shape[0] // tm
    row = lambda w: pl.BlockSpec((tm, w), lambda i: (i, 0))
    full = lambda a: pl.BlockSpec(a.shape, lambda i: (0,) * a.ndim)
    tab = pl.BlockSpec((tm, ROPE_PAD), lambda i: (i % nblk, 0))
    widths = (D_MODEL, D_MODEL, Q_LORA, KV_LORA, QK_ROPE, QK_WIDTH, None, ATTN_WIDTH, D_MODEL, D_MODEL)
    dtypes = (F32, F32, F32, F32, F32, BF16, BF16, F32, F32, F32)
    vt_spec = pl.BlockSpec((1, KV_LORA, tm), lambda i: (i, 0, 0))
    shape = lambda w: (n, w) if w else (n // tm, KV_LORA, tm)
    return pl.pallas_call(
        _in_proj_kernel,
        grid=(n // tm,),
        in_specs=[row(D_MODEL), full(gin), full(w_all), full(gkv), tab, tab],
        out_specs=[row(w) if w else vt_spec for w in widths],
        out_shape=[jax.ShapeDtypeStruct(shape(w), d) for w, d in zip(widths, dtypes)],
        compiler_params=_params("parallel"),
        name="in_proj",
    )(x, gin, w_all, gkv, cos_k, sin_k)


def _q_prep_kernel(cq_ref, gq_ref, wn_ref, wr_ref, wrs_ref, wuk_ref, cos_ref, sin_ref, q_ref):
    nb, _, t, _ = q_ref.shape
    cqn = _rms(cq_ref[...], gq_ref[...]).astype(BF16)
    qn = _dot(cqn, wn_ref[...]).astype(BF16)
    qr = _dot(cqn, wr_ref[...])
    qrs = _dot(cqn, wrs_ref[...])
    cos = cos_ref[...]
    sin = sin_ref[...]
    for h in range(N_HEADS):
        sl = slice(h * LANE, (h + 1) * LANE)
        q_lat = _dot(qn[:, sl], wuk_ref[h])
        q_rope = qr[:, sl] * cos + qrs[:, sl] * sin
        q_ref[:, h, :, :KV_LORA] = q_lat.reshape(nb, t, KV_LORA).astype(q_ref.dtype)
        q_ref[:, h, :, KV_LORA:] = q_rope.reshape(nb, t, ROPE_PAD).astype(q_ref.dtype)


def _q_prep(cq, gq, wn, wr, wrs, wuk, cos_q, sin_q, tm, batch, seq, out_dtype):
    n = cq.shape[0]
    nblk = cos_q.shape[0] // tm
    full = lambda a: pl.BlockSpec(a.shape, lambda i: (0,) * a.ndim)
    tab = pl.BlockSpec((tm, ROPE_PAD), lambda i: (i % nblk, 0))
    if seq >= tm:
        per = seq // tm
        out_spec = pl.BlockSpec((1, N_HEADS, tm, QK_WIDTH), lambda i: (i // per, 0, i % per, 0))
    else:
        out_spec = pl.BlockSpec((tm // seq, N_HEADS, seq, QK_WIDTH), lambda i: (i, 0, 0, 0))
    return pl.pallas_call(
        _q_prep_kernel,
        grid=(n // tm,),
        in_specs=[pl.BlockSpec((tm, Q_LORA), lambda i: (i, 0)), full(gq), full(wn), full(wr), full(wrs),
                  full(wuk), tab, tab],
        out_specs=out_spec,
        out_shape=jax.ShapeDtypeStruct((batch, N_HEADS, seq, QK_WIDTH), out_dtype),
        compiler_params=_params("parallel"),
        name="q_prep",
    )(cq, gq, wn, wr, wrs, wuk, cos_q, sin_q)


def _q_prep_t_kernel(cq_ref, gq_ref, wn_ref, wr_ref, wrs_ref, wuk_ref, cos_ref, sin_ref, q_ref):
    tm = cq_ref.shape[0]
    cqn = _rms(cq_ref[...], gq_ref[...]).astype(BF16)
    qn = _dot_nt(wn_ref[...], cqn).astype(BF16)
    qr = _dot_nt(wr_ref[...], cqn)
    qrs = _dot_nt(wrs_ref[...], cqn)
    cos = cos_ref[...]
    sin = sin_ref[...]
    for h in range(N_HEADS):
        sl = slice(h * LANE, (h + 1) * LANE)
        cl = slice(h * tm, (h + 1) * tm)
        q_ref[0, 0, :KV_LORA, cl] = _dot(wuk_ref[h], qn[sl]).astype(BF16)
        q_ref[0, 0, KV_LORA:, cl] = (qr[sl] * cos + qrs[sl] * sin).astype(BF16)


def _q_prep_t(cq, gq, wn_t, wr_t, wrs_t, wuk_t, cos_qt, sin_qt, tm, batch, seq):
    n = cq.shape[0]
    per = seq // tm
    full = lambda a: pl.BlockSpec(a.shape, lambda i: (0,) * a.ndim)
    tab = pl.BlockSpec((ROPE_PAD, tm), lambda i: (0, i % per))
    return pl.pallas_call(
        _q_prep_t_kernel,
        grid=(n // tm,),
        in_specs=[pl.BlockSpec((tm, Q_LORA), lambda i: (i, 0)), full(gq), full(wn_t), full(wr_t), full(wrs_t),
                  full(wuk_t), tab, tab],
        out_specs=pl.BlockSpec((1, 1, QK_WIDTH, N_HEADS * tm), lambda i: (i // per, i % per, 0, 0)),
        out_shape=jax.ShapeDtypeStruct((batch, per, QK_WIDTH, N_HEADS * tm), BF16),
        compiler_params=_params("parallel"),
        name="q_prep_t",
    )(cq, gq, wn_t, wr_t, wrs_t, wuk_t, cos_qt, sin_qt)


def _softmax_step(s, v, m_ref, l_ref, acc_ref):
    m_prev = m_ref[...]
    m_new = jnp.maximum(m_prev, jnp.max(s, axis=-1, keepdims=True))
    alpha = jnp.exp(m_prev - m_new)
    p = jnp.exp(s - m_new)
    l_ref[...] = alpha * l_ref[...] + jnp.sum(p, axis=-1, keepdims=True)
    acc_ref[...] = alpha * acc_ref[...] + _dot(p.astype(BF16), v)
    m_ref[...] = m_new


def _value_up(o_lat, wuv_ref, t):
    cols = []
    for j in range(N_HEADS // 2):
        a = o_lat[(2 * j) * t:(2 * j + 1) * t].astype(BF16)
        b = o_lat[(2 * j + 1) * t:(2 * j + 2) * t].astype(BF16)
        cols.append(_dot(a, wuv_ref[2 * j]) + _dot(b, wuv_ref[2 * j + 1]))
    return jnp.concatenate(cols, axis=-1)


def _init_softmax(m_ref, l_ref, acc_ref):
    m_ref[...] = jnp.full(m_ref.shape, -jnp.inf, F32)
    l_ref[...] = jnp.zeros(l_ref.shape, F32)
    acc_ref[...] = jnp.zeros(acc_ref.shape, F32)


def _prompt_attn_kernel(qt_ref, k_ref, vt_ref, wuvt_ref, o_ref, m_ref, l_ref, acc_ref, *, tq, cg):
    qi = pl.program_id(1)
    cols = N_HEADS * tq
    _init_softmax(m_ref, l_ref, acc_ref)

    def chunk(kb, diagonal):
        k = k_ref[0, pl.ds(pl.multiple_of(kb * tq, tq), tq), :]
        vt = vt_ref[kb]
        if diagonal:
            key = lax.broadcasted_iota(jnp.int32, (tq, cg), 0)
            tok = lax.broadcasted_iota(jnp.int32, (tq, cg), 1) & (tq - 1)
            causal = key <= tok
        for g in range(cols // cg):
            sl = slice(g * cg, (g + 1) * cg)
            s = _dot(k, qt_ref[0, 0, :, sl]) * ATTN_SCALE
            if diagonal:
                s = jnp.where(causal, s, NEG_INF)
            m_prev = m_ref[:, sl]
            m_new = jnp.maximum(m_prev, jnp.max(s, axis=0, keepdims=True))
            alpha = jnp.exp(m_prev - m_new)
            p = jnp.exp(s - m_new)
            l_ref[:, sl] = alpha * l_ref[:, sl] + jnp.sum(p, axis=0, keepdims=True)
            acc_ref[:, sl] = alpha * acc_ref[:, sl] + _dot(vt, p.astype(BF16))
            m_ref[:, sl] = m_new

    def body(kb, carry):
        chunk(kb, False)
        return carry

    lax.fori_loop(0, qi, body, 0)
    chunk(qi, True)
    heads = []
    for h in range(N_HEADS):
        sl = slice(h * tq, (h + 1) * tq)
        o_lat = (acc_ref[:, sl] / l_ref[:, sl]).astype(BF16)
        heads.append(_dot(wuvt_ref[h], o_lat))
    o_ref[0] = jnp.concatenate(heads, axis=0).T


def _prompt_attn(qt, kf, vt, wuv_t, tq, cg):
    b, nq, _, cols = qt.shape
    l = nq * tq
    return pl.pallas_call(
        functools.partial(_prompt_attn_kernel, tq=tq, cg=cg),
        grid=(b, nq),
        in_specs=[pl.BlockSpec((1, 1, QK_WIDTH, cols), lambda bi, qi: (bi, qi, 0, 0)),
                  pl.BlockSpec((1, l, QK_WIDTH), lambda bi, qi: (bi, 0, 0)),
                  pl.BlockSpec((nq, KV_LORA, tq), lambda bi, qi: (bi, 0, 0)),
                  pl.BlockSpec(wuv_t.shape, lambda bi, qi: (0, 0, 0))],
        out_specs=pl.BlockSpec((1, tq, ATTN_WIDTH), lambda bi, qi: (bi, qi, 0)),
        out_shape=jax.ShapeDtypeStruct((b, l, ATTN_WIDTH), F32),
        scratch_shapes=[pltpu.VMEM((1, cols), F32), pltpu.VMEM((1, cols), F32),
                        pltpu.VMEM((KV_LORA, cols), F32)],
        compiler_params=_params("parallel", "arbitrary"),
        name="prompt_attn",
    )(qt, kf, vt, wuv_t)


def _paged_attn_kernel(pt_ref, q_ref, ks_ref, rs_ref, wuv_ref, ckv_hbm, kr_hbm, o_ref,
                       m_ref, l_ref, acc_ref, kbuf, rbuf, sem, *, pps, page, t):
    b = pl.program_id(0)
    j = pl.program_id(1)
    nj = pl.num_programs(1)
    step = b * nj + j
    slot = step % 2
    rows = N_HEADS * t

    def page_copies(bb, jj, sl):
        for i in range(pps):
            pg = pt_ref[bb, jj * pps + i]
            yield pltpu.make_async_copy(ckv_hbm.at[pg], kbuf.at[sl, i], sem.at[0, sl])
            yield pltpu.make_async_copy(kr_hbm.at[pg], rbuf.at[sl, i], sem.at[1, sl])

    @pl.when(step == 0)
    def _():
        for cp in page_copies(b, j, slot):
            cp.start()

    @pl.when(step + 1 < pl.num_programs(0) * nj)
    def _():
        last = j == nj - 1
        for cp in page_copies(jnp.where(last, b + 1, b), jnp.where(last, 0, j + 1), 1 - slot):
            cp.start()

    q = q_ref[0].astype(BF16)
    q_lat = q[:, :KV_LORA]
    q_rope = q[:, KV_LORA:KV_LORA + QK_ROPE]

    def scores(k, r):
        return (_dot_nt(q_lat, k) + _dot_nt(q_rope, r)) * ATTN_SCALE

    @pl.when(j == 0)
    def _():
        _init_softmax(m_ref, l_ref, acc_ref)
        k = jnp.concatenate([ks_ref[0], jnp.zeros((page - t, KV_LORA), F32)], axis=0).astype(BF16)
        r = jnp.concatenate([rs_ref[0], jnp.zeros((page - t, QK_ROPE), F32)], axis=0).astype(BF16)
        s = scores(k, r)
        tok = lax.broadcasted_iota(jnp.int32, (t, page), 0)
        key = lax.broadcasted_iota(jnp.int32, (t, page), 1)
        s = jnp.where((key <= tok)[None], s.reshape(N_HEADS, t, page), NEG_INF).reshape(rows, page)
        _softmax_step(s, k, m_ref, l_ref, acc_ref)

    for cp in page_copies(b, j, slot):
        cp.wait()
    k = kbuf[slot].reshape(pps * page, KV_LORA).astype(BF16)
    r = rbuf[slot].reshape(pps * page, QK_ROPE).astype(BF16)
    _softmax_step(scores(k, r), k, m_ref, l_ref, acc_ref)

    @pl.when(j == nj - 1)
    def _():
        o_ref[0] = _value_up(acc_ref[...] / l_ref[...], wuv_ref, t)


def _paged_attn(q, ckv_new, kr_new, cache_ckv, cache_kr, page_table, wuv, pps):
    bd, rows, _ = q.shape
    t = rows // N_HEADS
    _, page, _ = cache_ckv.shape
    n_pages = page_table.shape[1]
    assert n_pages % pps == 0 and t <= page
    grid_spec = pltpu.PrefetchScalarGridSpec(
        num_scalar_prefetch=1,
        grid=(bd, n_pages // pps),
        in_specs=[pl.BlockSpec((1, rows, QK_WIDTH), lambda b, j, pt: (b, 0, 0)),
                  pl.BlockSpec((1, t, KV_LORA), lambda b, j, pt: (b, 0, 0)),
                  pl.BlockSpec((1, t, QK_ROPE), lambda b, j, pt: (b, 0, 0)),
                  pl.BlockSpec(wuv.shape, lambda b, j, pt: (0, 0, 0)),
                  pl.BlockSpec(memory_space=pl.ANY), pl.BlockSpec(memory_space=pl.ANY)],
        out_specs=pl.BlockSpec((1, t, ATTN_WIDTH), lambda b, j, pt: (b, 0, 0)),
        scratch_shapes=[pltpu.VMEM((rows, 1), F32), pltpu.VMEM((rows, 1), F32),
                        pltpu.VMEM((rows, KV_LORA), F32),
                        pltpu.VMEM((2, pps, page, KV_LORA), F32), pltpu.VMEM((2, pps, page, QK_ROPE), F32),
                        pltpu.SemaphoreType.DMA((2, 2))],
    )
    return pl.pallas_call(
        functools.partial(_paged_attn_kernel, pps=pps, page=page, t=t),
        grid_spec=grid_spec,
        out_shape=jax.ShapeDtypeStruct((bd, t, ATTN_WIDTH), F32),
        compiler_params=_params("arbitrary", "arbitrary"),
        name="paged_attn",
    )(page_table, q, ckv_new, kr_new, wuv, cache_ckv, cache_kr)


SET_GROUPS = LANE // SSM_GROUP
N_SETS = SSM_GROUPS // SET_GROUPS
SET_STATE = SET_GROUPS * SSM_STATE


def _cmul(xr, xi, ar, ai):
    return xr * ar - xi * ai, xr * ai + xi * ar


def _s5_kernel(*refs, tc, nc, has_h0):
    if has_h0:
        u_ref, ws_ref, bz_ref, cz_ref, pw_ref, h0_ref, y_ref, hl_ref = refs
    else:
        u_ref, ws_ref, bz_ref, cz_ref, pw_ref, y_ref, hl_ref = refs
    rows = u_ref.shape[0] // tc
    u_cat = jnp.concatenate([u_ref[pl.ds(s, rows, stride=tc), :].astype(BF16) for s in range(tc)], axis=1)
    z = _dot(u_cat, bz_ref[0])
    sr, si = z[:, :SET_STATE], z[:, SET_STATE:]
    if has_h0:
        assert nc == 1
        pr, pi = h0_ref[0, 0, :, :SET_STATE], h0_ref[0, 0, :, SET_STATE:]
        dr, di = _cmul(pr, pi, pw_ref[0, 0, 0:1], pw_ref[0, 0, 1:2])
        sr, si = sr + dr, si + di
    else:
        chunk = lax.broadcasted_iota(jnp.int32, (rows, SET_STATE), 0) & (nc - 1)
        for k in range(nc.bit_length() - 1):
            sh = 1 << k
            keep = chunk >= sh
            tr = jnp.where(keep, pltpu.roll(sr, sh, axis=0), 0.0)
            ti = jnp.where(keep, pltpu.roll(si, sh, axis=0), 0.0)
            dr, di = _cmul(tr, ti, pw_ref[0, k, 0:1], pw_ref[0, k, 1:2])
            sr, si = sr + dr, si + di
        pr = jnp.where(chunk >= 1, pltpu.roll(sr, 1, axis=0), 0.0)
        pi = jnp.where(chunk >= 1, pltpu.roll(si, 1, axis=0), 0.0)
    y_state = _dot(jnp.concatenate([pr, pi], axis=1).astype(BF16), cz_ref[0])
    for t in range(tc):
        y_t = y_state[:, t * LANE:(t + 1) * LANE] + _dot(u_cat[:, :(t + 1) * LANE], ws_ref[0, (tc - 1 - t) * LANE:, :])
        y_ref[pl.ds(t, rows, stride=tc), :] = y_t
    s_last = jnp.concatenate([sr, si], axis=1)
    if nc == 1:
        hl_ref[0, 0] = s_last
    else:
        for q in range(rows // nc):
            hl_ref[0, 0, q:q + 1, :] = s_last[q * nc + nc - 1:q * nc + nc, :]


def _s5(u, ws, bz, cz, pw, h0, tc, nc, block_rows):
    n = u.shape[0]
    nblk = n // block_rows
    nseq = block_rows // (tc * nc)
    per_set = lambda a: pl.BlockSpec((1,) + a.shape[1:], lambda si, rb: (si,) + (0,) * (a.ndim - 1))
    tok = pl.BlockSpec((block_rows, LANE), lambda si, rb: (rb, si))
    state = pl.BlockSpec((1, 1, nseq, 2 * SET_STATE), lambda si, rb: (si, rb, 0, 0))
    ins = [u, ws, bz, cz, pw] + ([h0] if h0 is not None else [])
    return pl.pallas_call(
        functools.partial(_s5_kernel, tc=tc, nc=nc, has_h0=h0 is not None),
        grid=(N_SETS, nblk),
        in_specs=[tok, per_set(ws), per_set(bz), per_set(cz), per_set(pw)] + ([state] if h0 is not None else []),
        out_specs=[tok, state],
        out_shape=[jax.ShapeDtypeStruct((n, SSM_GROUPS * SSM_GROUP), F32),
                   jax.ShapeDtypeStruct((N_SETS, nblk, nseq, 2 * SET_STATE), F32)],
        compiler_params=_params("parallel", "parallel"),
        name="s5",
    )(*ins)


def _block_diag(x):
    a, b = x.shape[-2:]
    eye = jnp.eye(SET_GROUPS, dtype=x.dtype)
    y = x[..., :, :, None, :] * eye[:, None, :, None]
    return y.reshape(x.shape[:-3] + (SET_GROUPS * a, SET_GROUPS * b))


def _s5_weights(lam_re, lam_im, log_dt, b_re, b_im, c_re, c_im, tc, n_levels):
    hi = lax.Precision.HIGHEST
    lam = lax.complex(lam_re.astype(F32), lam_im.astype(F32))
    ldt = lam * jnp.exp(log_dt.astype(F32))[:, None]
    lam_bar = jnp.exp(ldt)
    b_bar = ((lam_bar - 1.0) / lam)[..., None] * lax.complex(b_re.astype(F32), b_im.astype(F32))
    c = lax.complex(c_re.astype(F32), c_im.astype(F32))
    sets = lambda x: x.reshape(x.shape[:-3] + (N_SETS, SET_GROUPS) + x.shape[-2:])
    steps = jnp.arange(tc + 1, dtype=F32)
    pw = jnp.exp(ldt[None] * steps[:, None, None])
    kern = jnp.real(jnp.einsum('gop,dgp,gpi->dgio', c, pw[:tc], b_bar, precision=hi))
    ws = _block_diag(sets(kern[::-1])).transpose(1, 0, 2, 3).reshape(N_SETS, tc * LANE, LANE)
    bzc = (pw[tc - 1 - jnp.arange(tc)][..., None] * b_bar[None]).transpose(0, 1, 3, 2)
    bz = jnp.concatenate([_block_diag(sets(jnp.real(bzc))), _block_diag(sets(jnp.imag(bzc)))], axis=-1)
    bz = bz.transpose(1, 0, 2, 3).reshape(N_SETS, tc * LANE, 2 * SET_STATE)
    czc = (c[None] * pw[1:tc + 1][:, :, None, :]).transpose(0, 1, 3, 2)
    cz = jnp.concatenate([_block_diag(sets(jnp.real(czc))), _block_diag(sets(-jnp.imag(czc)))], axis=-2)
    cz = cz.transpose(1, 2, 0, 3).reshape(N_SETS, 2 * SET_STATE, tc * LANE)
    lev = jnp.exp(ldt[None] * (tc * 2.0 ** jnp.arange(n_levels, dtype=F32))[:, None, None])
    pwl = jnp.stack([jnp.real(lev), jnp.imag(lev)], axis=1).reshape(n_levels, 2, N_SETS, SET_STATE)
    return ws.astype(BF16), bz.astype(BF16), cz.astype(BF16), pwl.transpose(2, 0, 1, 3)


def _s5_branch(u, h0, ssm_w, batch, seq, tc):
    nc = seq // tc
    n_levels = max(nc.bit_length() - 1, 1)
    assert nc & (nc - 1) == 0 and (h0 is None or nc == 1)
    ws, bz, cz, pwl = _s5_weights(*ssm_w, tc, n_levels)
    block_rows = seq if nc > 1 else batch * seq
    nseq = block_rows // seq
    h0_t = None
    if h0 is not None:
        h0_t = h0.astype(F32).reshape(batch // nseq, nseq, N_SETS, SET_GROUPS, SSM_STATE, 2)
        h0_t = h0_t.transpose(2, 0, 1, 5, 3, 4).reshape(N_SETS, batch // nseq, nseq, 2 * SET_STATE)
    y, hl = _s5(u, ws, bz, cz, pwl, h0_t, tc, nc, block_rows)
    hl = hl.reshape(N_SETS, batch, 2, SET_GROUPS, SSM_STATE).transpose(1, 0, 3, 4, 2)
    return y, hl.reshape(batch, SSM_GROUPS, SSM_STATE, 2)


def _merge_kernel(x_ref, y_ref, u_ref, gs_ref, o_ref, ga_ref, ms_ref, ma_ref,
                  d_ref, wglu_ref, bglu_ref, wbs_ref, wba_ref, wout_ref, gf_ref, out_ref):
    y = y_ref[...] + d_ref[...] * u_ref[...]
    zg = jax.nn.gelu(y)
    glu = zg * jax.nn.sigmoid(_dot(zg.astype(BF16), wglu_ref[...]) + bglu_ref[...])
    y_s = _dot((glu * jax.nn.silu(gs_ref[...])).astype(BF16), wbs_ref[...])
    y_a = _dot((o_ref[...] * jax.nn.silu(ga_ref[...])).astype(BF16), wba_ref[...])
    merged = jax.nn.sigmoid(ms_ref[...]) * y_s + jax.nn.sigmoid(ma_ref[...]) * y_a
    h = x_ref[...] + _dot(merged.astype(BF16), wout_ref[...])
    out_ref[...] = _rms(h, gf_ref[...])


def _merge(x, y, u, gs, o, ga, ms, ma, d, wglu, bglu, wbs, wba, wout, gf, tm):
    n = x.shape[0]
    row = pl.BlockSpec((tm, D_MODEL), lambda i: (i, 0))
    full = lambda a: pl.BlockSpec(a.shape, lambda i: (0,) * a.ndim)
    consts = (d, wglu, bglu, wbs, wba, wout, gf)
    return pl.pallas_call(
        _merge_kernel,
        grid=(n // tm,),
        in_specs=[row] * 8 + [full(a) for a in consts],
        out_specs=row,
        out_shape=jax.ShapeDtypeStruct((n, D_MODEL), F32),
        compiler_params=_params("parallel"),
        name="merge",
    )(x, y, u, gs, o, ga, ms, ma, *consts)


def _rope_tables(pos, reps):
    half = QK_ROPE // 2
    inv = ROPE_BASE ** (-jnp.arange(half, dtype=F32) * (2.0 / QK_ROPE))
    ang = pos.astype(F32)[:, None] * inv[None, :]
    cos, sin = jnp.cos(ang), jnp.sin(ang)
    cos_t = _pad_last(jnp.concatenate([cos, cos], axis=-1), ROPE_PAD)
    sin_t = _pad_last(jnp.concatenate([-sin, sin], axis=-1), ROPE_PAD)
    return jnp.tile(cos_t, (reps, 1)), jnp.tile(sin_t, (reps, 1))


def _swap_halves(w):
    half = QK_ROPE // 2
    return jnp.concatenate([w[..., half:], w[..., :half]], axis=-1)


def _pad_last(w, width):
    return jnp.pad(w, [(0, 0)] * (w.ndim - 1) + [(0, width - w.shape[-1])])


def _rope_weight(w_rope):
    return _pad_last(w_rope, ROPE_PAD).reshape(Q_LORA, N_HEADS * ROPE_PAD).astype(BF16)


def _pack_weights(w_in, mla_w_uq, mla_w_uk, mla_w_uv):
    b = [0]
    for w in (D_MODEL, D_MODEL, Q_LORA, KV_LORA, QK_ROPE, ATTN_WIDTH, D_MODEL, D_MODEL):
        b.append(b[-1] + w)
    w_kr = w_in[:, b[4]:b[5]]
    w_all = jnp.concatenate([w_in[:, :b[4]], _pad_last(w_kr, ROPE_PAD), _pad_last(_swap_halves(w_kr), ROPE_PAD),
                             w_in[:, b[5]:]], axis=-1).astype(BF16)
    w_nope = _pad_last(mla_w_uq[..., :QK_NOPE], LANE).reshape(Q_LORA, N_HEADS * LANE).astype(BF16)
    w_rope = mla_w_uq[..., QK_NOPE:]
    w_uk = jnp.pad(mla_w_uk.transpose(1, 2, 0), ((0, 0), (0, LANE - QK_NOPE), (0, 0))).astype(BF16)
    w_uv = mla_w_uv.transpose(1, 0, 2)
    lo = jnp.pad(w_uv, ((0, 0), (0, 0), (0, LANE - V_DIM)))
    hi = jnp.pad(w_uv, ((0, 0), (0, 0), (LANE - V_DIM, 0)))
    w_uv_pair = jnp.where((jnp.arange(N_HEADS) % 2 == 0)[:, None, None], lo, hi).astype(BF16)
    return dict(
        w_all=w_all, w_nope=w_nope, w_uk=w_uk, w_uv_pair=w_uv_pair,
        w_r=_rope_weight(w_rope), w_rs=_rope_weight(_swap_halves(w_rope)),
        w_uv_t=w_uv.transpose(0, 2, 1).astype(BF16))


def _layer(x, pos, h0, attend, w, batch, seq, tc, tm):
    n = x.shape[0]
    cos_t, sin_t = _rope_tables(pos, max(tm // seq, 1))
    row = lambda v: v.reshape(1, -1).astype(F32)
    u, gs, cq, ckv, kr, kf, vt, ga, ms, ma = _in_proj(x, row(w["norm_in"]), w["w_all"], row(w["mla_kv_norm"]),
                                                      cos_t, sin_t, tm)
    o = attend(cq, cos_t, sin_t, ckv, kr, kf, vt).reshape(n, ATTN_WIDTH)
    y_ssm, state = _s5_branch(u, h0, w["ssm"], batch, seq, tc)
    y = _merge(x, y_ssm, u, gs, o, ga, ms, ma, row(w["ssm_d"]), w["ssm_w_glu"].astype(BF16), row(w["ssm_b_glu"]),
               w["w_br_ssm"].astype(BF16), w["w_br_attn"].astype(BF16), w["w_out"].astype(BF16),
               row(w["norm_final"]), tm)
    return y, ckv, kr, state


def kernel(x_prompt, x_sample, cache_ckv, cache_krope, state_ssm, page_table,
           norm_in, w_in, ssm_lambda_re, ssm_lambda_im, ssm_log_dt, ssm_b_re, ssm_b_im,
           ssm_c_re, ssm_c_im, ssm_d, ssm_w_glu, ssm_b_glu, w_br_ssm,
           mla_q_norm, mla_w_uq, mla_kv_norm, mla_w_uk, mla_w_uv, w_br_attn, w_out, norm_final):
    b, l, _ = x_prompt.shape
    bd, t, _ = x_sample.shape
    w = dict(norm_in=norm_in, mla_kv_norm=mla_kv_norm,
             ssm=(ssm_lambda_re, ssm_lambda_im, ssm_log_dt, ssm_b_re, ssm_b_im, ssm_c_re, ssm_c_im),
             ssm_d=ssm_d, ssm_w_glu=ssm_w_glu, ssm_b_glu=ssm_b_glu, w_br_ssm=w_br_ssm,
             w_br_attn=w_br_attn, w_out=w_out, norm_final=norm_final,
             **_pack_weights(w_in, mla_w_uq, mla_w_uk, mla_w_uv))
    tm = 256
    gq = mla_q_norm.reshape(1, -1).astype(F32)

    def attend_prompt(cq, cos_t, sin_t, ckv, kr, kf, vt):
        qt = _q_prep_t(cq, gq, w["w_nope"].T, w["w_r"].T, w["w_rs"].T, w["w_uk"].transpose(0, 2, 1),
                       cos_t.T, sin_t.T, tm, b, l)
        return _prompt_attn(qt, kf.reshape(b, l, QK_WIDTH), vt, w["w_uv_t"], tm, cg=4096)

    def attend_sample(cq, cos_t, sin_t, ckv, kr, kf, vt):
        q = _q_prep(cq, gq, w["w_nope"], w["w_r"], w["w_rs"], w["w_uk"], cos_t, sin_t, tm, bd, t, F32)
        return _paged_attn(q.reshape(bd, N_HEADS * t, QK_WIDTH), ckv.reshape(bd, t, KV_LORA),
                           kr.reshape(bd, t, QK_ROPE), cache_ckv, cache_krope, page_table, w["w_uv_pair"],
                           pps=32)

    y_p, ckv_p, kr_p, ssm_p = _layer(x_prompt.reshape(b * l, D_MODEL), jnp.arange(l), None, attend_prompt,
                                     w, b, l, 16, tm)
    y_s, ckv_s, kr_s, ssm_s = _layer(x_sample.reshape(bd * t, D_MODEL), PAST_LEN + jnp.arange(t), state_ssm,
                                     attend_sample, w, bd, t, t, tm)
    return (y_p.reshape(b, l, D_MODEL), y_s.reshape(bd, t, D_MODEL),
            ckv_p.reshape(b, l, KV_LORA), kr_p.reshape(b, l, QK_ROPE), ssm_p,
            ckv_s.reshape(bd, t, KV_LORA), kr_s.reshape(bd, t, QK_ROPE), ssm_s)
```

```python
import functools
import math

import jax
import jax.numpy as jnp
from jax import lax
from jax.experimental import pallas as pl
from jax.experimental.pallas import tpu as pltpu

D_MODEL = 1024
PAST_LEN = 16384
SSM_GROUP = 16
SSM_GROUPS = 64
SSM_STATE = 64
N_HEADS = 16
QK_NOPE = 64
QK_ROPE = 32
V_DIM = 64
Q_LORA = 384
KV_LORA = 256
ATTN_WIDTH = N_HEADS * V_DIM
ROPE_BASE = 10000.0
ATTN_SCALE = (QK_NOPE + QK_ROPE) ** -0.5
NEG_INF = -1e30
NORM_EPS = 1e-6
LOG2_E = math.log2(math.e)

LANE = 128
ROPE_PAD = LANE
QK_WIDTH = KV_LORA + ROPE_PAD
VMEM_LIMIT = 56 * 1024 * 1024

F32 = jnp.float32
BF16 = jnp.bfloat16

_C_U = 0
_C_GS = _C_U + D_MODEL
_C_CQ = _C_GS + D_MODEL
_C_CKV = _C_CQ + Q_LORA
_C_KR = _C_CKV + KV_LORA
_C_KRS = _C_KR + ROPE_PAD
_C_GA = _C_KRS + ROPE_PAD
_C_MS = _C_GA + ATTN_WIDTH
_C_MA = _C_MS + D_MODEL
_C_END = _C_MA + D_MODEL


def _rms(x, g):
    return x * lax.rsqrt(jnp.mean(x * x, axis=-1, keepdims=True) + NORM_EPS) * g


def _dot(a, b):
    return jnp.dot(a, b, preferred_element_type=F32)


def _dot_nt(a, b):
    return lax.dot_general(a, b, (((1,), (1,)), ((), ())), preferred_element_type=F32)


def _params(*sem):
    return pltpu.CompilerParams(dimension_semantics=sem, vmem_limit_bytes=VMEM_LIMIT)


def _in_proj_kernel(x_ref, gin_ref, w_ref, gkv_ref, cos_ref, sin_ref,
                    u_ref, gs_ref, cq_ref, ckv_ref, kr_ref, kf_ref, vt_ref, ga_ref, ms_ref, ma_ref):
    xn = _rms(x_ref[...], gin_ref[...]).astype(BF16)

    def proj(lo, hi):
        return _dot(xn, w_ref[:, lo:hi])

    u_ref[...] = proj(_C_U, _C_GS)
    gs_ref[...] = proj(_C_GS, _C_CQ)
    cq_ref[...] = proj(_C_CQ, _C_CKV)
    ckv = _rms(proj(_C_CKV, _C_KR), gkv_ref[...])
    ckv_ref[...] = ckv
    kr = proj(_C_KR, _C_KRS) * cos_ref[...] + proj(_C_KRS, _C_GA) * sin_ref[...]
    kr_ref[...] = kr[:, :QK_ROPE]
    kf_ref[:, :KV_LORA] = ckv.astype(BF16)
    kf_ref[:, KV_LORA:] = kr.astype(BF16)
    vt_ref[0] = ckv.T.astype(BF16)
    ga_ref[...] = proj(_C_GA, _C_MS)
    ms_ref[...] = proj(_C_MS, _C_MA)
    ma_ref[...] = proj(_C_MA, _C_END)


def _in_proj(x, gin, w_all, gkv, cos_k, sin_k, tm):
    n = x.shape[0]
    nblk = cos_k.shape[0] // tm
    row = lambda w: pl.BlockSpec((tm, w), lambda i: (i, 0))
    full = lambda a: pl.BlockSpec(a.shape, lambda i: (0,) * a.ndim)
    tab = pl.BlockSpec((tm, ROPE_PAD), lambda i: (i % nblk, 0))
    widths = (D_MODEL, D_MODEL, Q_LORA, KV_LORA, QK_ROPE, QK_WIDTH, None, ATTN_WIDTH, D_MODEL, D_MODEL)
    dtypes = (F32, F32, F32, F32, F32, BF16, BF16, F32, F32, F32)
    vt_spec = pl.BlockSpec((1, KV_LORA, tm), lambda i: (i, 0, 0))
    shape = lambda w: (n, w) if w else (n // tm, KV_LORA, tm)
    return pl.pallas_call(
        _in_proj_kernel,
        grid=(n // tm,),
        in_specs=[row(D_MODEL), full(gin), full(w_all), full(gkv), tab, tab],
        out_specs=[row(w) if w else vt_spec for w in widths],
        out_shape=[jax.ShapeDtypeStruct(shape(w), d) for w, d in zip(widths, dtypes)],
        compiler_params=_params("parallel"),
        name="in_proj",
    )(x, gin, w_all, gkv, cos_k, sin_k)


def _q_prep_kernel(cq_ref, gq_ref, wn_ref, wr_ref, wrs_ref, wuk_ref, cos_ref, sin_ref, q_ref):
    nb, _, t, _ = q_ref.shape
    cqn = _rms(cq_ref[...], gq_ref[...]).astype(BF16)
    qn = _dot(cqn, wn_ref[...]).astype(BF16)
    qr = _dot(cqn, wr_ref[...])
    qrs = _dot(cqn, wrs_ref[...])
    cos = cos_ref[...]
    sin = sin_ref[...]
    for h in range(N_HEADS):
        sl = slice(h * LANE, (h + 1) * LANE)
        q_lat = _dot(qn[:, sl], wuk_ref[h])
        q_rope = qr[:, sl] * cos + qrs[:, sl] * sin
        q_ref[:, h, :, :KV_LORA] = q_lat.reshape(nb, t, KV_LORA).astype(q_ref.dtype)
        q_ref[:, h, :, KV_LORA:] = q_rope.reshape(nb, t, ROPE_PAD).astype(q_ref.dtype)


def _q_prep(cq, gq, wn, wr, wrs, wuk, cos_q, sin_q, tm, batch, seq, out_dtype):
    n = cq.shape[0]
    nblk = cos_q.shape[0] // tm
    full = lambda a: pl.BlockSpec(a.shape, lambda i: (0,) * a.ndim)
    tab = pl.BlockSpec((tm, ROPE_PAD), lambda i: (i % nblk, 0))
    if seq >= tm:
        per = seq // tm
        out_spec = pl.BlockSpec((1, N_HEADS, tm, QK_WIDTH), lambda i: (i // per, 0, i % per, 0))
    else:
        out_spec = pl.BlockSpec((tm // seq, N_HEADS, seq, QK_WIDTH), lambda i: (i, 0, 0, 0))
    return pl.pallas_call(
        _q_prep_kernel,
        grid=(n // tm,),
        in_specs=[pl.BlockSpec((tm, Q_LORA), lambda i: (i, 0)), full(gq), full(wn), full(wr), full(wrs),
                  full(wuk), tab, tab],
        out_specs=out_spec,
        out_shape=jax.ShapeDtypeStruct((batch, N_HEADS, seq, QK_WIDTH), out_dtype),
        compiler_params=_params("parallel"),
        name="q_prep",
    )(cq, gq, wn, wr, wrs, wuk, cos_q, sin_q)


def _q_prep_t_kernel(cq_ref, gq_ref, wn_ref, wr_ref, wrs_ref, wuk_ref, cos_ref, sin_ref, q_ref):
    tm = cq_ref.shape[0]
    cqn = _rms(cq_ref[...], gq_ref[...]).astype(BF16)
    qn = _dot_nt(wn_ref[...], cqn).astype(BF16)
    qr = _dot_nt(wr_ref[...], cqn)
    qrs = _dot_nt(wrs_ref[...], cqn)
    cos = cos_ref[...]
    sin = sin_ref[...]
    for h in range(N_HEADS):
        sl = slice(h * LANE, (h + 1) * LANE)
        cl = slice(h * tm, (h + 1) * tm)
        q_ref[0, 0, :KV_LORA, cl] = _dot(wuk_ref[h], qn[sl]).astype(BF16)
        q_ref[0, 0, KV_LORA:, cl] = (qr[sl] * cos + qrs[sl] * sin).astype(BF16)


def _q_prep_t(cq, gq, wn_t, wr_t, wrs_t, wuk_t, cos_qt, sin_qt, tm, batch, seq):
    n = cq.shape[0]
    per = seq // tm
    full = lambda a: pl.BlockSpec(a.shape, lambda i: (0,) * a.ndim)
    tab = pl.BlockSpec((ROPE_PAD, tm), lambda i: (0, i % per))
    return pl.pallas_call(
        _q_prep_t_kernel,
        grid=(n // tm,),
        in_specs=[pl.BlockSpec((tm, Q_LORA), lambda i: (i, 0)), full(gq), full(wn_t), full(wr_t), full(wrs_t),
                  full(wuk_t), tab, tab],
        out_specs=pl.BlockSpec((1, 1, QK_WIDTH, N_HEADS * tm), lambda i: (i // per, i % per, 0, 0)),
        out_shape=jax.ShapeDtypeStruct((batch, per, QK_WIDTH, N_HEADS * tm), BF16),
        compiler_params=_params("parallel"),
        name="q_prep_t",
    )(cq, gq, wn_t, wr_t, wrs_t, wuk_t, cos_qt, sin_qt)


def _softmax_step(s, v, m_ref, l_ref, acc_ref):
    m_prev = m_ref[...]
    m_new = jnp.maximum(m_prev, jnp.max(s, axis=-1, keepdims=True))
    alpha = jnp.exp(m_prev - m_new)
    p = jnp.exp(s - m_new)
    l_ref[...] = alpha * l_ref[...] + jnp.sum(p, axis=-1, keepdims=True)
    acc_ref[...] = alpha * acc_ref[...] + _dot(p.astype(BF16), v)
    m_ref[...] = m_new


def _value_up(o_lat, wuv_ref, t):
    cols = []
    for j in range(N_HEADS // 2):
        a = o_lat[(2 * j) * t:(2 * j + 1) * t].astype(BF16)
        b = o_lat[(2 * j + 1) * t:(2 * j + 2) * t].astype(BF16)
        cols.append(_dot(a, wuv_ref[2 * j]) + _dot(b, wuv_ref[2 * j + 1]))
    return jnp.concatenate(cols, axis=-1)


def _init_softmax(m_ref, l_ref, acc_ref):
    m_ref[...] = jnp.full(m_ref.shape, -jnp.inf, F32)
    l_ref[...] = jnp.zeros(l_ref.shape, F32)
    acc_ref[...] = jnp.zeros(acc_ref.shape, F32)


def _prompt_attn_kernel(qt_ref, k_ref, vt_ref, wuvt_ref, o_ref, m_ref, l_ref, acc_ref, *, tq, cg):
    qi = pl.program_id(1)
    cols = N_HEADS * tq
    _init_softmax(m_ref, l_ref, acc_ref)

    def chunk(kb, diagonal):
        k = k_ref[0, pl.ds(pl.multiple_of(kb * tq, tq), tq), :]
        vt = vt_ref[kb]
        if diagonal:
            key = lax.broadcasted_iota(jnp.int32, (tq, cg), 0)
            tok = lax.broadcasted_iota(jnp.int32, (tq, cg), 1) & (tq - 1)
            causal = key <= tok
        for g in range(cols // cg):
            sl = slice(g * cg, (g + 1) * cg)
            s = _dot(k, qt_ref[0, 0, :, sl]) * (ATTN_SCALE * LOG2_E)
            if diagonal:
                s = jnp.where(causal, s, NEG_INF)
            m_prev = m_ref[:, sl]
            m_new = jnp.maximum(m_prev, jnp.max(s, axis=0, keepdims=True))
            alpha = jnp.exp2(m_prev - m_new)
            p = jnp.exp2(s - m_new)
            l_ref[:, sl] = alpha * l_ref[:, sl] + jnp.sum(p, axis=0, keepdims=True)
            acc_ref[:, sl] = alpha * acc_ref[:, sl] + _dot(vt, p.astype(BF16))
            m_ref[:, sl] = m_new

    def body(kb, carry):
        chunk(kb, False)
        return carry

    lax.fori_loop(0, qi, body, 0)
    chunk(qi, True)
    heads = []
    for h in range(N_HEADS):
        sl = slice(h * tq, (h + 1) * tq)
        o_lat = (acc_ref[:, sl] / l_ref[:, sl]).astype(BF16)
        heads.append(_dot(wuvt_ref[h], o_lat))
    o_ref[0] = jnp.concatenate(heads, axis=0).T


def _prompt_attn(qt, kf, vt, wuv_t, tq, cg):
    b, nq, _, cols = qt.shape
    l = nq * tq
    return pl.pallas_call(
        functools.partial(_prompt_attn_kernel, tq=tq, cg=cg),
        grid=(b, nq),
        in_specs=[pl.BlockSpec((1, 1, QK_WIDTH, cols), lambda bi, qi: (bi, qi, 0, 0)),
                  pl.BlockSpec((1, l, QK_WIDTH), lambda bi, qi: (bi, 0, 0)),
                  pl.BlockSpec((nq, KV_LORA, tq), lambda bi, qi: (bi, 0, 0)),
                  pl.BlockSpec(wuv_t.shape, lambda bi, qi: (0, 0, 0))],
        out_specs=pl.BlockSpec((1, tq, ATTN_WIDTH), lambda bi, qi: (bi, qi, 0)),
        out_shape=jax.ShapeDtypeStruct((b, l, ATTN_WIDTH), F32),
        scratch_shapes=[pltpu.VMEM((1, cols), F32), pltpu.VMEM((1, cols), F32),
                        pltpu.VMEM((KV_LORA, cols), F32)],
        compiler_params=_params("parallel", "arbitrary"),
        name="prompt_attn",
    )(qt, kf, vt, wuv_t)


def _paged_attn_kernel(pt_ref, q_ref, ks_ref, rs_ref, wuv_ref, ckv_hbm, kr_hbm, o_ref,
                       m_ref, l_ref, acc_ref, kbuf, rbuf, sem, *, pps, page, t):
    b = pl.program_id(0)
    j = pl.program_id(1)
    nj = pl.num_programs(1)
    step = b * nj + j
    slot = step % 2
    rows = N_HEADS * t

    def page_copies(bb, jj, sl):
        for i in range(pps):
            pg = pt_ref[bb, jj * pps + i]
            yield pltpu.make_async_copy(ckv_hbm.at[pg], kbuf.at[sl, i], sem.at[0, sl])
            yield pltpu.make_async_copy(kr_hbm.at[pg], rbuf.at[sl, i], sem.at[1, sl])

    @pl.when(step == 0)
    def _():
        for cp in page_copies(b, j, slot):
            cp.start()

    @pl.when(step + 1 < pl.num_programs(0) * nj)
    def _():
        last = j == nj - 1
        for cp in page_copies(jnp.where(last, b + 1, b), jnp.where(last, 0, j + 1), 1 - slot):
            cp.start()

    q = q_ref[0].astype(BF16)
    q_lat = q[:, :KV_LORA]
    q_rope = q[:, KV_LORA:KV_LORA + QK_ROPE]

    @pl.when(j == 0)
    def _():
        _init_softmax(m_ref, l_ref, acc_ref)
        k = jnp.concatenate([ks_ref[0], jnp.zeros((page - t, KV_LORA), F32)], axis=0).astype(BF16)
        r = jnp.concatenate([rs_ref[0], jnp.zeros((page - t, QK_ROPE), F32)], axis=0).astype(BF16)
        s = (_dot_nt(q_lat, k) + _dot_nt(q_rope, r)) * ATTN_SCALE
        tok = lax.broadcasted_iota(jnp.int32, (t, page), 0)
        key = lax.broadcasted_iota(jnp.int32, (t, page), 1)
        s = jnp.where((key <= tok)[None], s.reshape(N_HEADS, t, page), NEG_INF).reshape(rows, page)
        _softmax_step(s, k, m_ref, l_ref, acc_ref)

    for cp in page_copies(b, j, slot):
        cp.wait()
    k = kbuf[slot].reshape(pps * page, KV_LORA).astype(BF16)
    r_t = jnp.concatenate([rbuf[slot, i] for i in range(pps)], axis=1).astype(BF16)
    s = (_dot_nt(q_lat, k) + _dot(q_rope, r_t)) * ATTN_SCALE
    _softmax_step(s, k, m_ref, l_ref, acc_ref)

    @pl.when(j == nj - 1)
    def _():
        o_ref[0] = _value_up(acc_ref[...] / l_ref[...], wuv_ref, t)


def _paged_attn(q, ckv_new, kr_new, cache_ckv, cache_kr, page_table, wuv, pps):
    bd, rows, _ = q.shape
    t = rows // N_HEADS
    _, page, _ = cache_ckv.shape
    n_pages = page_table.shape[1]
    assert n_pages % pps == 0 and t <= page
    grid_spec = pltpu.PrefetchScalarGridSpec(
        num_scalar_prefetch=1,
        grid=(bd, n_pages // pps),
        in_specs=[pl.BlockSpec((1, rows, QK_WIDTH), lambda b, j, pt: (b, 0, 0)),
                  pl.BlockSpec((1, t, KV_LORA), lambda b, j, pt: (b, 0, 0)),
                  pl.BlockSpec((1, t, QK_ROPE), lambda b, j, pt: (b, 0, 0)),
                  pl.BlockSpec(wuv.shape, lambda b, j, pt: (0, 0, 0)),
                  pl.BlockSpec(memory_space=pl.ANY), pl.BlockSpec(memory_space=pl.ANY)],
        out_specs=pl.BlockSpec((1, t, ATTN_WIDTH), lambda b, j, pt: (b, 0, 0)),
        scratch_shapes=[pltpu.VMEM((rows, 1), F32), pltpu.VMEM((rows, 1), F32),
                        pltpu.VMEM((rows, KV_LORA), F32),
                        pltpu.VMEM((2, pps, page, KV_LORA), F32), pltpu.VMEM((2, pps, QK_ROPE, page), F32),
                        pltpu.SemaphoreType.DMA((2, 2))],
    )
    return pl.pallas_call(
        functools.partial(_paged_attn_kernel, pps=pps, page=page, t=t),
        grid_spec=grid_spec,
        out_shape=jax.ShapeDtypeStruct((bd, t, ATTN_WIDTH), F32),
        compiler_params=_params("arbitrary", "arbitrary"),
        name="paged_attn",
    )(page_table, q, ckv_new, kr_new, wuv, cache_ckv, cache_kr.transpose(0, 2, 1))


SET_GROUPS = LANE // SSM_GROUP
N_SETS = SSM_GROUPS // SET_GROUPS
SET_STATE = SET_GROUPS * SSM_STATE


def _cmul(xr, xi, ar, ai):
    return xr * ar - xi * ai, xr * ai + xi * ar


def _s5_kernel(*refs, tc, nc, has_h0):
    if has_h0:
        u_ref, ws_ref, bz_ref, cz_ref, pw_ref, h0_ref, y_ref, hl_ref = refs
    else:
        u_ref, ws_ref, bz_ref, cz_ref, pw_ref, y_ref, hl_ref = refs
    rows = u_ref.shape[0] // tc
    u_cat = jnp.concatenate([u_ref[pl.ds(s, rows, stride=tc), :].astype(BF16) for s in range(tc)], axis=1)
    z = _dot(u_cat, bz_ref[0])
    sr, si = z[:, :SET_STATE], z[:, SET_STATE:]
    if has_h0:
        assert nc == 1
        pr, pi = h0_ref[0, 0, :, :SET_STATE], h0_ref[0, 0, :, SET_STATE:]
        dr, di = _cmul(pr, pi, pw_ref[0, 0, 0:1], pw_ref[0, 0, 1:2])
        sr, si = sr + dr, si + di
    else:
        chunk = lax.broadcasted_iota(jnp.int32, (rows, SET_STATE), 0) & (nc - 1)
        for k in range(nc.bit_length() - 1):
            sh = 1 << k
            keep = chunk >= sh
            tr = jnp.where(keep, pltpu.roll(sr, sh, axis=0), 0.0)
            ti = jnp.where(keep, pltpu.roll(si, sh, axis=0), 0.0)
            dr, di = _cmul(tr, ti, pw_ref[0, k, 0:1], pw_ref[0, k, 1:2])
            sr, si = sr + dr, si + di
        pr = jnp.where(chunk >= 1, pltpu.roll(sr, 1, axis=0), 0.0)
        pi = jnp.where(chunk >= 1, pltpu.roll(si, 1, axis=0), 0.0)
    y_state = _dot(jnp.concatenate([pr, pi], axis=1).astype(BF16), cz_ref[0])
    for t in range(tc):
        y_t = y_state[:, t * LANE:(t + 1) * LANE] + _dot(u_cat[:, :(t + 1) * LANE], ws_ref[0, (tc - 1 - t) * LANE:, :])
        y_ref[pl.ds(t, rows, stride=tc), :] = y_t
    s_last = jnp.concatenate([sr, si], axis=1)
    if nc == 1:
        hl_ref[0, 0] = s_last
    else:
        for q in range(rows // nc):
            hl_ref[0, 0, q:q + 1, :] = s_last[q * nc + nc - 1:q * nc + nc, :]


def _s5(u, ws, bz, cz, pw, h0, tc, nc, block_rows):
    n = u.shape[0]
    nblk = n // block_rows
    nseq = block_rows // (tc * nc)
    per_set = lambda a: pl.BlockSpec((1,) + a.shape[1:], lambda si, rb: (si,) + (0,) * (a.ndim - 1))
    tok = pl.BlockSpec((block_rows, LANE), lambda si, rb: (rb, si))
    state = pl.BlockSpec((1, 1, nseq, 2 * SET_STATE), lambda si, rb: (si, rb, 0, 0))
    ins = [u, ws, bz, cz, pw] + ([h0] if h0 is not None else [])
    return pl.pallas_call(
        functools.partial(_s5_kernel, tc=tc, nc=nc, has_h0=h0 is not None),
        grid=(N_SETS, nblk),
        in_specs=[tok, per_set(ws), per_set(bz), per_set(cz), per_set(pw)] + ([state] if h0 is not None else []),
        out_specs=[tok, state],
        out_shape=[jax.ShapeDtypeStruct((n, SSM_GROUPS * SSM_GROUP), F32),
                   jax.ShapeDtypeStruct((N_SETS, nblk, nseq, 2 * SET_STATE), F32)],
        compiler_params=_params("parallel", "parallel"),
        name="s5",
    )(*ins)


def _block_diag(x):
    a, b = x.shape[-2:]
    rows = x.reshape(x.shape[:-3] + (SET_GROUPS * a, b))
    wide = jnp.concatenate([rows] * SET_GROUPS, axis=-1)
    shape = (SET_GROUPS * a, SET_GROUPS * b)
    same = lax.broadcasted_iota(jnp.int32, shape, 0) // a == lax.broadcasted_iota(jnp.int32, shape, 1) // b
    return jnp.where(same, wide, 0)


def _s5_weights(lam_re, lam_im, log_dt, b_re, b_im, c_re, c_im, tc, n_levels):
    hi = lax.Precision.HIGHEST
    lam = lax.complex(lam_re.astype(F32), lam_im.astype(F32))
    ldt = lam * jnp.exp(log_dt.astype(F32))[:, None]
    lam_bar = jnp.exp(ldt)
    b_bar = ((lam_bar - 1.0) / lam)[..., None] * lax.complex(b_re.astype(F32), b_im.astype(F32))
    c = lax.complex(c_re.astype(F32), c_im.astype(F32))
    sets = lambda x: x.reshape(x.shape[:-3] + (N_SETS, SET_GROUPS) + x.shape[-2:])
    steps = jnp.arange(tc + 1, dtype=F32)
    pw = jnp.exp(ldt[None] * steps[:, None, None])
    kern = jnp.real(jnp.einsum('gop,dgp,gpi->dgio', c, pw[:tc], b_bar, precision=hi))
    ws = _block_diag(sets(kern[::-1])).astype(BF16).transpose(1, 0, 2, 3).reshape(N_SETS, tc * LANE, LANE)
    bzc = (pw[tc - 1 - jnp.arange(tc)][..., None] * b_bar[None]).transpose(0, 1, 3, 2)
    bz = jnp.concatenate([_block_diag(sets(jnp.real(bzc))), _block_diag(sets(jnp.imag(bzc)))], axis=-1)
    bz = bz.astype(BF16).transpose(1, 0, 2, 3).reshape(N_SETS, tc * LANE, 2 * SET_STATE)
    czc = (c[None] * pw[1:tc + 1][:, :, None, :]).transpose(0, 1, 3, 2)
    cz = jnp.concatenate([_block_diag(sets(jnp.real(czc))), _block_diag(sets(-jnp.imag(czc)))], axis=-2)
    cz = jnp.concatenate([cz[t] for t in range(tc)], axis=-1).astype(BF16)
    lev = jnp.exp(ldt[None] * (tc * 2.0 ** jnp.arange(n_levels, dtype=F32))[:, None, None])
    pwl = jnp.stack([jnp.real(lev), jnp.imag(lev)], axis=1).reshape(n_levels, 2, N_SETS, SET_STATE)
    return ws, bz, cz, pwl.transpose(2, 0, 1, 3)


def _s5_branch(u, h0, ssm_w, batch, seq, tc):
    nc = seq // tc
    n_levels = max(nc.bit_length() - 1, 1)
    assert nc & (nc - 1) == 0 and (h0 is None or nc == 1)
    ws, bz, cz, pwl = _s5_weights(*ssm_w, tc, n_levels)
    block_rows = seq if nc > 1 else batch * seq
    nseq = block_rows // seq
    h0_t = None
    if h0 is not None:
        h0_t = h0.astype(F32).reshape(batch // nseq, nseq, N_SETS, SET_GROUPS, SSM_STATE, 2)
        h0_t = h0_t.transpose(2, 0, 1, 5, 3, 4).reshape(N_SETS, batch // nseq, nseq, 2 * SET_STATE)
    y, hl = _s5(u, ws, bz, cz, pwl, h0_t, tc, nc, block_rows)
    hl = hl.reshape(N_SETS, batch, 2, SET_GROUPS, SSM_STATE).transpose(1, 0, 3, 4, 2)
    return y, hl.reshape(batch, SSM_GROUPS, SSM_STATE, 2)


def _merge_kernel(x_ref, y_ref, u_ref, gs_ref, o_ref, ga_ref, ms_ref, ma_ref,
                  d_ref, wglu_ref, bglu_ref, wbs_ref, wba_ref, wout_ref, gf_ref, out_ref):
    y = y_ref[...] + d_ref[...] * u_ref[...]
    zg = jax.nn.gelu(y)
    glu = zg * jax.nn.sigmoid(_dot(zg.astype(BF16), wglu_ref[...]) + bglu_ref[...])
    y_s = _dot((glu * jax.nn.silu(gs_ref[...])).astype(BF16), wbs_ref[...])
    y_a = _dot((o_ref[...] * jax.nn.silu(ga_ref[...])).astype(BF16), wba_ref[...])
    merged = jax.nn.sigmoid(ms_ref[...]) * y_s + jax.nn.sigmoid(ma_ref[...]) * y_a
    h = x_ref[...] + _dot(merged.astype(BF16), wout_ref[...])
    out_ref[...] = _rms(h, gf_ref[...])


def _merge(x, y, u, gs, o, ga, ms, ma, d, wglu, bglu, wbs, wba, wout, gf, tm):
    n = x.shape[0]
    row = pl.BlockSpec((tm, D_MODEL), lambda i: (i, 0))
    full = lambda a: pl.BlockSpec(a.shape, lambda i: (0,) * a.ndim)
    consts = (d, wglu, bglu, wbs, wba, wout, gf)
    return pl.pallas_call(
        _merge_kernel,
        grid=(n // tm,),
        in_specs=[row] * 8 + [full(a) for a in consts],
        out_specs=row,
        out_shape=jax.ShapeDtypeStruct((n, D_MODEL), F32),
        compiler_params=_params("parallel"),
        name="merge",
    )(x, y, u, gs, o, ga, ms, ma, *consts)


def _rope_tables(pos, reps):
    half = QK_ROPE // 2
    inv = ROPE_BASE ** (-jnp.arange(half, dtype=F32) * (2.0 / QK_ROPE))
    ang = pos.astype(F32)[:, None] * inv[None, :]
    cos, sin = jnp.cos(ang), jnp.sin(ang)
    cos_t = _pad_last(jnp.concatenate([cos, cos], axis=-1), ROPE_PAD)
    sin_t = _pad_last(jnp.concatenate([-sin, sin], axis=-1), ROPE_PAD)
    return jnp.tile(cos_t, (reps, 1)), jnp.tile(sin_t, (reps, 1))


def _swap_halves(w):
    half = QK_ROPE // 2
    return jnp.concatenate([w[..., half:], w[..., :half]], axis=-1)


def _pad_last(w, width):
    return jnp.pad(w, [(0, 0)] * (w.ndim - 1) + [(0, width - w.shape[-1])])


def _rope_weight(w_rope):
    return _pad_last(w_rope, ROPE_PAD).reshape(Q_LORA, N_HEADS * ROPE_PAD).astype(BF16)


def _pack_weights(w_in, mla_w_uq, mla_w_uk, mla_w_uv):
    b = [0]
    for w in (D_MODEL, D_MODEL, Q_LORA, KV_LORA, QK_ROPE, ATTN_WIDTH, D_MODEL, D_MODEL):
        b.append(b[-1] + w)
    w_kr = w_in[:, b[4]:b[5]]
    w_all = jnp.concatenate([w_in[:, :b[4]], _pad_last(w_kr, ROPE_PAD), _pad_last(_swap_halves(w_kr), ROPE_PAD),
                             w_in[:, b[5]:]], axis=-1).astype(BF16)
    w_nope = _pad_last(mla_w_uq[..., :QK_NOPE], LANE).reshape(Q_LORA, N_HEADS * LANE).astype(BF16)
    w_rope = mla_w_uq[..., QK_NOPE:]
    w_uk = jnp.pad(mla_w_uk.transpose(1, 2, 0), ((0, 0), (0, LANE - QK_NOPE), (0, 0))).astype(BF16)
    w_uv = mla_w_uv.transpose(1, 0, 2)
    lo = jnp.pad(w_uv, ((0, 0), (0, 0), (0, LANE - V_DIM)))
    hi = jnp.pad(w_uv, ((0, 0), (0, 0), (LANE - V_DIM, 0)))
    w_uv_pair = jnp.where((jnp.arange(N_HEADS) % 2 == 0)[:, None, None], lo, hi).astype(BF16)
    return dict(
        w_all=w_all, w_nope=w_nope, w_uk=w_uk, w_uv_pair=w_uv_pair,
        w_r=_rope_weight(w_rope), w_rs=_rope_weight(_swap_halves(w_rope)),
        w_uv_t=w_uv.transpose(0, 2, 1).astype(BF16))


def _layer(x, pos, h0, attend, w, batch, seq, tc, tm):
    n = x.shape[0]
    cos_t, sin_t = _rope_tables(pos, max(tm // seq, 1))
    row = lambda v: v.reshape(1, -1).astype(F32)
    u, gs, cq, ckv, kr, kf, vt, ga, ms, ma = _in_proj(x, row(w["norm_in"]), w["w_all"], row(w["mla_kv_norm"]),
                                                      cos_t, sin_t, tm)
    o = attend(cq, cos_t, sin_t, ckv, kr, kf, vt).reshape(n, ATTN_WIDTH)
    y_ssm, state = _s5_branch(u, h0, w["ssm"], batch, seq, tc)
    y = _merge(x, y_ssm, u, gs, o, ga, ms, ma, row(w["ssm_d"]), w["ssm_w_glu"].astype(BF16), row(w["ssm_b_glu"]),
               w["w_br_ssm"].astype(BF16), w["w_br_attn"].astype(BF16), w["w_out"].astype(BF16),
               row(w["norm_final"]), tm)
    return y, ckv, kr, state


def kernel(x_prompt, x_sample, cache_ckv, cache_krope, state_ssm, page_table,
           norm_in, w_in, ssm_lambda_re, ssm_lambda_im, ssm_log_dt, ssm_b_re, ssm_b_im,
           ssm_c_re, ssm_c_im, ssm_d, ssm_w_glu, ssm_b_glu, w_br_ssm,
           mla_q_norm, mla_w_uq, mla_kv_norm, mla_w_uk, mla_w_uv, w_br_attn, w_out, norm_final):
    b, l, _ = x_prompt.shape
    bd, t, _ = x_sample.shape
    w = dict(norm_in=norm_in, mla_kv_norm=mla_kv_norm,
             ssm=(ssm_lambda_re, ssm_lambda_im, ssm_log_dt, ssm_b_re, ssm_b_im, ssm_c_re, ssm_c_im),
             ssm_d=ssm_d, ssm_w_glu=ssm_w_glu, ssm_b_glu=ssm_b_glu, w_br_ssm=w_br_ssm,
             w_br_attn=w_br_attn, w_out=w_out, norm_final=norm_final,
             **_pack_weights(w_in, mla_w_uq, mla_w_uk, mla_w_uv))
    tm = 256
    gq = mla_q_norm.reshape(1, -1).astype(F32)

    def attend_prompt(cq, cos_t, sin_t, ckv, kr, kf, vt):
        qt = _q_prep_t(cq, gq, w["w_nope"].T, w["w_r"].T, w["w_rs"].T, w["w_uk"].transpose(0, 2, 1),
                       cos_t.T, sin_t.T, tm, b, l)
        return _prompt_attn(qt, kf.reshape(b, l, QK_WIDTH), vt, w["w_uv_t"], tm, cg=4096)

    def attend_sample(cq, cos_t, sin_t, ckv, kr, kf, vt):
        q = _q_prep(cq, gq, w["w_nope"], w["w_r"], w["w_rs"], w["w_uk"], cos_t, sin_t, tm, bd, t, F32)
        return _paged_attn(q.reshape(bd, N_HEADS * t, QK_WIDTH), ckv.reshape(bd, t, KV_LORA),
                           kr.reshape(bd, t, QK_ROPE), cache_ckv, cache_krope, page_table, w["w_uv_pair"],
                           pps=32)

    y_p, ckv_p, kr_p, ssm_p = _layer(x_prompt.reshape(b * l, D_MODEL), jnp.arange(l), None, attend_prompt,
                                     w, b, l, 16, tm)
    y_s, ckv_s, kr_s, ssm_s = _layer(x_sample.reshape(bd * t, D_MODEL), PAST_LEN + jnp.arange(t), state_ssm,
                                     attend_sample, w, bd, t, t, tm)
    return (y_p.reshape(b, l, D_MODEL), y_s.reshape(bd, t, D_MODEL),
            ckv_p.reshape(b, l, KV_LORA), kr_p.reshape(b, l, QK_ROPE), ssm_p,
            ckv_s.reshape(bd, t, KV_LORA), kr_s.reshape(bd, t, QK_ROPE), ssm_s)
```

```python
import functools
import math

import jax
import jax.numpy as jnp
from jax import lax
from jax.experimental import pallas as pl
from jax.experimental.pallas import tpu as pltpu

D_MODEL = 1024
PAST_LEN = 16384
SSM_GROUP = 16
SSM_GROUPS = 64
SSM_STATE = 64
N_HEADS = 16
QK_NOPE = 64
QK_ROPE = 32
V_DIM = 64
Q_LORA = 384
KV_LORA = 256
ATTN_WIDTH = N_HEADS * V_DIM
ROPE_BASE = 10000.0
ATTN_SCALE = (QK_NOPE + QK_ROPE) ** -0.5
NEG_INF = -1e30
NORM_EPS = 1e-6
LOG2_E = math.log2(math.e)

LANE = 128
ROPE_PAD = LANE
QK_WIDTH = KV_LORA + ROPE_PAD
VMEM_LIMIT = 56 * 1024 * 1024

F32 = jnp.float32
BF16 = jnp.bfloat16

_C_U = 0
_C_GS = _C_U + D_MODEL
_C_CQ = _C_GS + D_MODEL
_C_CKV = _C_CQ + Q_LORA
_C_KR = _C_CKV + KV_LORA
_C_KRS = _C_KR + ROPE_PAD
_C_GA = _C_KRS + ROPE_PAD
_C_MS = _C_GA + ATTN_WIDTH
_C_MA = _C_MS + D_MODEL
_C_END = _C_MA + D_MODEL


def _rms(x, g):
    return x * lax.rsqrt(jnp.mean(x * x, axis=-1, keepdims=True) + NORM_EPS) * g


def _dot(a, b):
    return jnp.dot(a, b, preferred_element_type=F32)


def _dot_nt(a, b):
    return lax.dot_general(a, b, (((1,), (1,)), ((), ())), preferred_element_type=F32)


def _params(*sem):
    return pltpu.CompilerParams(dimension_semantics=sem, vmem_limit_bytes=VMEM_LIMIT)


def _in_proj_kernel(x_ref, gin_ref, w_ref, gkv_ref, cos_ref, sin_ref,
                    u_ref, gs_ref, cq_ref, ckv_ref, kr_ref, kf_ref, vt_ref, ga_ref, ms_ref, ma_ref):
    xn = _rms(x_ref[...], gin_ref[...]).astype(BF16)

    def proj(lo, hi):
        return _dot(xn, w_ref[:, lo:hi])

    u_ref[...] = proj(_C_U, _C_GS)
    gs_ref[...] = proj(_C_GS, _C_CQ)
    cq_ref[...] = proj(_C_CQ, _C_CKV)
    ckv = _rms(proj(_C_CKV, _C_KR), gkv_ref[...])
    ckv_ref[...] = ckv
    kr = proj(_C_KR, _C_KRS) * cos_ref[...] + proj(_C_KRS, _C_GA) * sin_ref[...]
    kr_ref[...] = kr[:, :QK_ROPE]
    kf_ref[:, :KV_LORA] = ckv.astype(BF16)
    kf_ref[:, KV_LORA:] = kr.astype(BF16)
    vt_ref[0] = ckv.T.astype(BF16)
    ga_ref[...] = proj(_C_GA, _C_MS)
    ms_ref[...] = proj(_C_MS, _C_MA)
    ma_ref[...] = proj(_C_MA, _C_END)


def _in_proj(x, gin, w_all, gkv, cos_k, sin_k, tm):
    n = x.shape[0]
    nblk = cos_k.shape[0] // tm
    row = lambda w: pl.BlockSpec((tm, w), lambda i: (i, 0))
    full = lambda a: pl.BlockSpec(a.shape, lambda i: (0,) * a.ndim)
    tab = pl.BlockSpec((tm, ROPE_PAD), lambda i: (i % nblk, 0))
    widths = (D_MODEL, D_MODEL, Q_LORA, KV_LORA, QK_ROPE, QK_WIDTH, None, ATTN_WIDTH, D_MODEL, D_MODEL)
    dtypes = (F32, F32, F32, F32, F32, BF16, BF16, F32, F32, F32)
    vt_spec = pl.BlockSpec((1, KV_LORA, tm), lambda i: (i, 0, 0))
    shape = lambda w: (n, w) if w else (n // tm, KV_LORA, tm)
    return pl.pallas_call(
        _in_proj_kernel,
        grid=(n // tm,),
        in_specs=[row(D_MODEL), full(gin), full(w_all), full(gkv), tab, tab],
        out_specs=[row(w) if w else vt_spec for w in widths],
        out_shape=[jax.ShapeDtypeStruct(shape(w), d) for w, d in zip(widths, dtypes)],
        compiler_params=_params("parallel"),
        name="in_proj",
    )(x, gin, w_all, gkv, cos_k, sin_k)


def _q_prep_kernel(cq_ref, gq_ref, wn_ref, wr_ref, wrs_ref, wuk_ref, cos_ref, sin_ref, q_ref):
    nb, _, t, _ = q_ref.shape
    cqn = _rms(cq_ref[...], gq_ref[...]).astype(BF16)
    qn = _dot(cqn, wn_ref[...]).astype(BF16)
    qr = _dot(cqn, wr_ref[...])
    qrs = _dot(cqn, wrs_ref[...])
    cos = cos_ref[...]
    sin = sin_ref[...]
    for h in range(N_HEADS):
        sl = slice(h * LANE, (h + 1) * LANE)
        q_lat = _dot(qn[:, sl], wuk_ref[h])
        q_rope = qr[:, sl] * cos + qrs[:, sl] * sin
        q_ref[:, h, :, :KV_LORA] = q_lat.reshape(nb, t, KV_LORA).astype(q_ref.dtype)
        q_ref[:, h, :, KV_LORA:] = q_rope.reshape(nb, t, ROPE_PAD).astype(q_ref.dtype)


def _q_prep(cq, gq, wn, wr, wrs, wuk, cos_q, sin_q, tm, batch, seq, out_dtype):
    n = cq.shape[0]
    nblk = cos_q.shape[0] // tm
    full = lambda a: pl.BlockSpec(a.shape, lambda i: (0,) * a.ndim)
    tab = pl.BlockSpec((tm, ROPE_PAD), lambda i: (i % nblk, 0))
    if seq >= tm:
        per = seq // tm
        out_spec = pl.BlockSpec((1, N_HEADS, tm, QK_WIDTH), lambda i: (i // per, 0, i % per, 0))
    else:
        out_spec = pl.BlockSpec((tm // seq, N_HEADS, seq, QK_WIDTH), lambda i: (i, 0, 0, 0))
    return pl.pallas_call(
        _q_prep_kernel,
        grid=(n // tm,),
        in_specs=[pl.BlockSpec((tm, Q_LORA), lambda i: (i, 0)), full(gq), full(wn), full(wr), full(wrs),
                  full(wuk), tab, tab],
        out_specs=out_spec,
        out_shape=jax.ShapeDtypeStruct((batch, N_HEADS, seq, QK_WIDTH), out_dtype),
        compiler_params=_params("parallel"),
        name="q_prep",
    )(cq, gq, wn, wr, wrs, wuk, cos_q, sin_q)


def _q_prep_t_kernel(cq_ref, gq_ref, wn_ref, wr_ref, wrs_ref, wuk_ref, cos_ref, sin_ref, q_ref):
    tm = cq_ref.shape[0]
    cqn = _rms(cq_ref[...], gq_ref[...]).astype(BF16)
    qn = _dot_nt(wn_ref[...], cqn).astype(BF16)
    qr = _dot_nt(wr_ref[...], cqn)
    qrs = _dot_nt(wrs_ref[...], cqn)
    cos = cos_ref[...]
    sin = sin_ref[...]
    for h in range(N_HEADS):
        sl = slice(h * LANE, (h + 1) * LANE)
        cl = slice(h * tm, (h + 1) * tm)
        q_ref[0, 0, :KV_LORA, cl] = _dot(wuk_ref[h], qn[sl]).astype(BF16)
        q_ref[0, 0, KV_LORA:, cl] = (qr[sl] * cos + qrs[sl] * sin).astype(BF16)


def _q_prep_t(cq, gq, wn_t, wr_t, wrs_t, wuk_t, cos_qt, sin_qt, tm, batch, seq):
    n = cq.shape[0]
    per = seq // tm
    full = lambda a: pl.BlockSpec(a.shape, lambda i: (0,) * a.ndim)
    tab = pl.BlockSpec((ROPE_PAD, tm), lambda i: (0, i % per))
    return pl.pallas_call(
        _q_prep_t_kernel,
        grid=(n // tm,),
        in_specs=[pl.BlockSpec((tm, Q_LORA), lambda i: (i, 0)), full(gq), full(wn_t), full(wr_t), full(wrs_t),
                  full(wuk_t), tab, tab],
        out_specs=pl.BlockSpec((1, 1, QK_WIDTH, N_HEADS * tm), lambda i: (i // per, i % per, 0, 0)),
        out_shape=jax.ShapeDtypeStruct((batch, per, QK_WIDTH, N_HEADS * tm), BF16),
        compiler_params=_params("parallel"),
        name="q_prep_t",
    )(cq, gq, wn_t, wr_t, wrs_t, wuk_t, cos_qt, sin_qt)


def _softmax_step(s, v, m_ref, l_ref, acc_ref):
    m_prev = m_ref[...]
    m_new = jnp.maximum(m_prev, jnp.max(s, axis=-1, keepdims=True))
    alpha = jnp.exp(m_prev - m_new)
    p = jnp.exp(s - m_new)
    l_ref[...] = alpha * l_ref[...] + jnp.sum(p, axis=-1, keepdims=True)
    acc_ref[...] = alpha * acc_ref[...] + _dot(p.astype(BF16), v)
    m_ref[...] = m_new


def _value_up(o_lat, wuv_ref, t):
    cols = []
    for j in range(N_HEADS // 2):
        a = o_lat[(2 * j) * t:(2 * j + 1) * t].astype(BF16)
        b = o_lat[(2 * j + 1) * t:(2 * j + 2) * t].astype(BF16)
        cols.append(_dot(a, wuv_ref[2 * j]) + _dot(b, wuv_ref[2 * j + 1]))
    return jnp.concatenate(cols, axis=-1)


def _init_softmax(m_ref, l_ref, acc_ref):
    m_ref[...] = jnp.full(m_ref.shape, -jnp.inf, F32)
    l_ref[...] = jnp.zeros(l_ref.shape, F32)
    acc_ref[...] = jnp.zeros(acc_ref.shape, F32)


def _prompt_attn_kernel(qt_ref, k_ref, vt_ref, wuvt_ref, o_ref, m_ref, l_ref, acc_ref, *, tq, cg):
    qi = pl.program_id(1)
    cols = N_HEADS * tq
    _init_softmax(m_ref, l_ref, acc_ref)

    def chunk(kb, diagonal):
        k = k_ref[0, pl.ds(pl.multiple_of(kb * tq, tq), tq), :]
        vt = vt_ref[kb]
        if diagonal:
            key = lax.broadcasted_iota(jnp.int32, (tq, cg), 0)
            tok = lax.broadcasted_iota(jnp.int32, (tq, cg), 1) & (tq - 1)
            causal = key <= tok
        for g in range(cols // cg):
            sl = slice(g * cg, (g + 1) * cg)
            s = _dot(k, qt_ref[0, 0, :, sl]) * (ATTN_SCALE * LOG2_E)
            if diagonal:
                s = jnp.where(causal, s, NEG_INF)
            m_prev = m_ref[:, sl]
            m_new = jnp.maximum(m_prev, jnp.max(s, axis=0, keepdims=True))
            alpha = jnp.exp2(m_prev - m_new)
            p = jnp.exp2(s - m_new)
            l_ref[:, sl] = alpha * l_ref[:, sl] + jnp.sum(p, axis=0, keepdims=True)
            acc_ref[:, sl] = alpha * acc_ref[:, sl] + _dot(vt, p.astype(BF16))
            m_ref[:, sl] = m_new

    def body(kb, carry):
        chunk(kb, False)
        return carry

    lax.fori_loop(0, qi, body, 0)
    chunk(qi, True)
    heads = []
    for h in range(N_HEADS):
        sl = slice(h * tq, (h + 1) * tq)
        o_lat = (acc_ref[:, sl] / l_ref[:, sl]).astype(BF16)
        heads.append(_dot(wuvt_ref[h], o_lat))
    o_ref[0] = jnp.concatenate(heads, axis=0).T


def _prompt_attn(qt, kf, vt, wuv_t, tq, cg):
    b, nq, _, cols = qt.shape
    l = nq * tq
    return pl.pallas_call(
        functools.partial(_prompt_attn_kernel, tq=tq, cg=cg),
        grid=(b, nq),
        in_specs=[pl.BlockSpec((1, 1, QK_WIDTH, cols), lambda bi, qi: (bi, qi, 0, 0)),
                  pl.BlockSpec((1, l, QK_WIDTH), lambda bi, qi: (bi, 0, 0)),
                  pl.BlockSpec((nq, KV_LORA, tq), lambda bi, qi: (bi, 0, 0)),
                  pl.BlockSpec(wuv_t.shape, lambda bi, qi: (0, 0, 0))],
        out_specs=pl.BlockSpec((1, tq, ATTN_WIDTH), lambda bi, qi: (bi, qi, 0)),
        out_shape=jax.ShapeDtypeStruct((b, l, ATTN_WIDTH), F32),
        scratch_shapes=[pltpu.VMEM((1, cols), F32), pltpu.VMEM((1, cols), F32),
                        pltpu.VMEM((KV_LORA, cols), F32)],
        compiler_params=_params("parallel", "arbitrary"),
        name="prompt_attn",
    )(qt, kf, vt, wuv_t)


def _paged_attn_kernel(pt_ref, q_ref, ks_ref, rs_ref, wuv_ref, ckv_hbm, kr_hbm, o_ref,
                       m_ref, l_ref, acc_ref, kbuf, rbuf, sem, *, pps, page, t):
    b = pl.program_id(0)
    j = pl.program_id(1)
    nj = pl.num_programs(1)
    step = b * nj + j
    slot = step % 2
    rows = N_HEADS * t

    def page_copies(bb, jj, sl):
        for i in range(pps):
            pg = pt_ref[bb, jj * pps + i]
            yield pltpu.make_async_copy(ckv_hbm.at[pg], kbuf.at[sl, i], sem.at[0, sl])
            yield pltpu.make_async_copy(kr_hbm.at[pg], rbuf.at[sl, i], sem.at[1, sl])

    @pl.when(step == 0)
    def _():
        for cp in page_copies(b, j, slot):
            cp.start()

    @pl.when(step + 1 < pl.num_programs(0) * nj)
    def _():
        last = j == nj - 1
        for cp in page_copies(jnp.where(last, b + 1, b), jnp.where(last, 0, j + 1), 1 - slot):
            cp.start()

    q = q_ref[0].astype(BF16)
    q_lat = q[:, :KV_LORA]
    q_rope = q[:, KV_LORA:KV_LORA + QK_ROPE]

    @pl.when(j == 0)
    def _():
        _init_softmax(m_ref, l_ref, acc_ref)
        k = jnp.concatenate([ks_ref[0], jnp.zeros((page - t, KV_LORA), F32)], axis=0).astype(BF16)
        r = jnp.concatenate([rs_ref[0], jnp.zeros((page - t, QK_ROPE), F32)], axis=0).astype(BF16)
        s = (_dot_nt(q_lat, k) + _dot_nt(q_rope, r)) * ATTN_SCALE
        tok = lax.broadcasted_iota(jnp.int32, (t, page), 0)
        key = lax.broadcasted_iota(jnp.int32, (t, page), 1)
        s = jnp.where((key <= tok)[None], s.reshape(N_HEADS, t, page), NEG_INF).reshape(rows, page)
        _softmax_step(s, k, m_ref, l_ref, acc_ref)

    for cp in page_copies(b, j, slot):
        cp.wait()
    k = kbuf[slot].reshape(pps * page, KV_LORA).astype(BF16)
    r_t = jnp.concatenate([rbuf[slot, i] for i in range(pps)], axis=1).astype(BF16)
    s = (_dot_nt(q_lat, k) + _dot(q_rope, r_t)) * ATTN_SCALE
    _softmax_step(s, k, m_ref, l_ref, acc_ref)

    @pl.when(j == nj - 1)
    def _():
        o_ref[0] = _value_up(acc_ref[...] / l_ref[...], wuv_ref, t)


def _paged_attn(q, ckv_new, kr_new, cache_ckv, cache_kr, page_table, wuv, pps):
    bd, rows, _ = q.shape
    t = rows // N_HEADS
    _, page, _ = cache_ckv.shape
    n_pages = page_table.shape[1]
    assert n_pages % pps == 0 and t <= page
    grid_spec = pltpu.PrefetchScalarGridSpec(
        num_scalar_prefetch=1,
        grid=(bd, n_pages // pps),
        in_specs=[pl.BlockSpec((1, rows, QK_WIDTH), lambda b, j, pt: (b, 0, 0)),
                  pl.BlockSpec((1, t, KV_LORA), lambda b, j, pt: (b, 0, 0)),
                  pl.BlockSpec((1, t, QK_ROPE), lambda b, j, pt: (b, 0, 0)),
                  pl.BlockSpec(wuv.shape, lambda b, j, pt: (0, 0, 0)),
                  pl.BlockSpec(memory_space=pl.ANY), pl.BlockSpec(memory_space=pl.ANY)],
        out_specs=pl.BlockSpec((1, t, ATTN_WIDTH), lambda b, j, pt: (b, 0, 0)),
        scratch_shapes=[pltpu.VMEM((rows, 1), F32), pltpu.VMEM((rows, 1), F32),
                        pltpu.VMEM((rows, KV_LORA), F32),
                        pltpu.VMEM((2, pps, page, KV_LORA), F32), pltpu.VMEM((2, pps, QK_ROPE, page), F32),
                        pltpu.SemaphoreType.DMA((2, 2))],
    )
    return pl.pallas_call(
        functools.partial(_paged_attn_kernel, pps=pps, page=page, t=t),
        grid_spec=grid_spec,
        out_shape=jax.ShapeDtypeStruct((bd, t, ATTN_WIDTH), F32),
        compiler_params=_params("arbitrary", "arbitrary"),
        name="paged_attn",
    )(page_table, q, ckv_new, kr_new, wuv, cache_ckv, cache_kr.transpose(0, 2, 1))


SET_GROUPS = LANE // SSM_GROUP
N_SETS = SSM_GROUPS // SET_GROUPS
SET_STATE = SET_GROUPS * SSM_STATE


def _cmul(xr, xi, ar, ai):
    return xr * ar - xi * ai, xr * ai + xi * ar


def _s5_kernel(*refs, tc, nc, has_h0):
    if has_h0:
        u_ref, ws_ref, bz_ref, cz_ref, pw_ref, h0_ref, y_ref, hl_ref = refs
    else:
        u_ref, ws_ref, bz_ref, cz_ref, pw_ref, y_ref, hl_ref = refs
    rows = u_ref.shape[0] // tc
    u_cat = jnp.concatenate([u_ref[pl.ds(s, rows, stride=tc), :].astype(BF16) for s in range(tc)], axis=1)
    z = _dot(u_cat, bz_ref[0])
    sr, si = z[:, :SET_STATE], z[:, SET_STATE:]
    if has_h0:
        assert nc == 1
        pr, pi = h0_ref[0, 0, :, :SET_STATE], h0_ref[0, 0, :, SET_STATE:]
        dr, di = _cmul(pr, pi, pw_ref[0, 0, 0:1], pw_ref[0, 0, 1:2])
        sr, si = sr + dr, si + di
    else:
        chunk = lax.broadcasted_iota(jnp.int32, (rows, SET_STATE), 0) & (nc - 1)
        for k in range(nc.bit_length() - 1):
            sh = 1 << k
            keep = chunk >= sh
            tr = jnp.where(keep, pltpu.roll(sr, sh, axis=0), 0.0)
            ti = jnp.where(keep, pltpu.roll(si, sh, axis=0), 0.0)
            dr, di = _cmul(tr, ti, pw_ref[0, k, 0:1], pw_ref[0, k, 1:2])
            sr, si = sr + dr, si + di
        pr = jnp.where(chunk >= 1, pltpu.roll(sr, 1, axis=0), 0.0)
        pi = jnp.where(chunk >= 1, pltpu.roll(si, 1, axis=0), 0.0)
    y_state = _dot(jnp.concatenate([pr, pi], axis=1).astype(BF16), cz_ref[0])
    for t in range(tc):
        y_t = y_state[:, t * LANE:(t + 1) * LANE] + _dot(u_cat[:, :(t + 1) * LANE], ws_ref[0, (tc - 1 - t) * LANE:, :])
        y_ref[pl.ds(t, rows, stride=tc), :] = y_t
    s_last = jnp.concatenate([sr, si], axis=1)
    if nc == 1:
        hl_ref[0, 0] = s_last
    else:
        for q in range(rows // nc):
            hl_ref[0, 0, q:q + 1, :] = s_last[q * nc + nc - 1:q * nc + nc, :]


def _s5(u, ws, bz, cz, pw, h0, tc, nc, block_rows):
    n = u.shape[0]
    nblk = n // block_rows
    nseq = block_rows // (tc * nc)
    per_set = lambda a: pl.BlockSpec((1,) + a.shape[1:], lambda si, rb: (si,) + (0,) * (a.ndim - 1))
    tok = pl.BlockSpec((block_rows, LANE), lambda si, rb: (rb, si))
    state = pl.BlockSpec((1, 1, nseq, 2 * SET_STATE), lambda si, rb: (si, rb, 0, 0))
    ins = [u, ws, bz, cz, pw] + ([h0] if h0 is not None else [])
    return pl.pallas_call(
        functools.partial(_s5_kernel, tc=tc, nc=nc, has_h0=h0 is not None),
        grid=(N_SETS, nblk),
        in_specs=[tok, per_set(ws), per_set(bz), per_set(cz), per_set(pw)] + ([state] if h0 is not None else []),
        out_specs=[tok, state],
        out_shape=[jax.ShapeDtypeStruct((n, SSM_GROUPS * SSM_GROUP), F32),
                   jax.ShapeDtypeStruct((N_SETS, nblk, nseq, 2 * SET_STATE), F32)],
        compiler_params=_params("parallel", "parallel"),
        name="s5",
    )(*ins)


def _block_diag(x):
    a, b = x.shape[-2:]
    rows = x.reshape(x.shape[:-3] + (SET_GROUPS * a, b)).astype(BF16)
    spread = lax.broadcasted_iota(jnp.int32, (b, SET_GROUPS * b), 1) % b == lax.broadcasted_iota(
        jnp.int32, (b, SET_GROUPS * b), 0)
    wide = jnp.dot(rows, spread.astype(BF16), preferred_element_type=BF16)
    shape = (SET_GROUPS * a, SET_GROUPS * b)
    same = lax.broadcasted_iota(jnp.int32, shape, 0) // a == lax.broadcasted_iota(jnp.int32, shape, 1) // b
    return jnp.where(same, wide, jnp.zeros((), BF16))


def _s5_weights(lam_re, lam_im, log_dt, b_re, b_im, c_re, c_im, tc, n_levels):
    hi = lax.Precision.HIGHEST
    lam = lax.complex(lam_re.astype(F32), lam_im.astype(F32))
    ldt = lam * jnp.exp(log_dt.astype(F32))[:, None]
    lam_bar = jnp.exp(ldt)
    b_bar = ((lam_bar - 1.0) / lam)[..., None] * lax.complex(b_re.astype(F32), b_im.astype(F32))
    c = lax.complex(c_re.astype(F32), c_im.astype(F32))
    sets = lambda x: x.reshape(x.shape[:-3] + (N_SETS, SET_GROUPS) + x.shape[-2:])
    steps = jnp.arange(tc + 1, dtype=F32)
    pw = jnp.exp(ldt[None] * steps[:, None, None])
    kern = jnp.real(jnp.einsum('gop,dgp,gpi->dgio', c, pw[:tc], b_bar, precision=hi))
    ws = _block_diag(sets(kern[::-1])).transpose(1, 0, 2, 3).reshape(N_SETS, tc * LANE, LANE)
    bzc = (pw[tc - 1 - jnp.arange(tc)][..., None] * b_bar[None]).transpose(0, 1, 3, 2)
    bz = jnp.concatenate([_block_diag(sets(jnp.real(bzc))), _block_diag(sets(jnp.imag(bzc)))], axis=-1)
    bz = bz.transpose(1, 0, 2, 3).reshape(N_SETS, tc * LANE, 2 * SET_STATE)
    czc = (c[None] * pw[1:tc + 1][:, :, None, :]).transpose(0, 1, 3, 2)
    cz = jnp.concatenate([_block_diag(sets(jnp.real(czc))), _block_diag(sets(-jnp.imag(czc)))], axis=-2)
    cz = jnp.concatenate([cz[t] for t in range(tc)], axis=-1)
    lev = jnp.exp(ldt[None] * (tc * 2.0 ** jnp.arange(n_levels, dtype=F32))[:, None, None])
    pwl = jnp.stack([jnp.real(lev), jnp.imag(lev)], axis=1).reshape(n_levels, 2, N_SETS, SET_STATE)
    return ws, bz, cz, pwl.transpose(2, 0, 1, 3)


def _s5_branch(u, h0, ssm_w, batch, seq, tc):
    nc = seq // tc
    n_levels = max(nc.bit_length() - 1, 1)
    assert nc & (nc - 1) == 0 and (h0 is None or nc == 1)
    ws, bz, cz, pwl = _s5_weights(*ssm_w, tc, n_levels)
    block_rows = seq if nc > 1 else batch * seq
    nseq = block_rows // seq
    h0_t = None
    if h0 is not None:
        h0_t = h0.astype(F32).reshape(batch // nseq, nseq, N_SETS, SET_GROUPS, SSM_STATE, 2)
        h0_t = h0_t.transpose(2, 0, 1, 5, 3, 4).reshape(N_SETS, batch // nseq, nseq, 2 * SET_STATE)
    y, hl = _s5(u, ws, bz, cz, pwl, h0_t, tc, nc, block_rows)
    hl = hl.reshape(N_SETS, batch, 2, SET_GROUPS, SSM_STATE).transpose(1, 0, 3, 4, 2)
    return y, hl.reshape(batch, SSM_GROUPS, SSM_STATE, 2)


def _merge_kernel(x_ref, y_ref, u_ref, gs_ref, o_ref, ga_ref, ms_ref, ma_ref,
                  d_ref, wglu_ref, bglu_ref, wbs_ref, wba_ref, wout_ref, gf_ref, out_ref):
    y = y_ref[...] + d_ref[...] * u_ref[...]
    zg = jax.nn.gelu(y)
    glu = zg * jax.nn.sigmoid(_dot(zg.astype(BF16), wglu_ref[...]) + bglu_ref[...])
    y_s = _dot((glu * jax.nn.silu(gs_ref[...])).astype(BF16), wbs_ref[...])
    y_a = _dot((o_ref[...] * jax.nn.silu(ga_ref[...])).astype(BF16), wba_ref[...])
    merged = jax.nn.sigmoid(ms_ref[...]) * y_s + jax.nn.sigmoid(ma_ref[...]) * y_a
    h = x_ref[...] + _dot(merged.astype(BF16), wout_ref[...])
    out_ref[...] = _rms(h, gf_ref[...])


def _merge(x, y, u, gs, o, ga, ms, ma, d, wglu, bglu, wbs, wba, wout, gf, tm):
    n = x.shape[0]
    row = pl.BlockSpec((tm, D_MODEL), lambda i: (i, 0))
    full = lambda a: pl.BlockSpec(a.shape, lambda i: (0,) * a.ndim)
    consts = (d, wglu, bglu, wbs, wba, wout, gf)
    return pl.pallas_call(
        _merge_kernel,
        grid=(n // tm,),
        in_specs=[row] * 8 + [full(a) for a in consts],
        out_specs=row,
        out_shape=jax.ShapeDtypeStruct((n, D_MODEL), F32),
        compiler_params=_params("parallel"),
        name="merge",
    )(x, y, u, gs, o, ga, ms, ma, *consts)


def _rope_tables(pos, reps):
    half = QK_ROPE // 2
    inv = ROPE_BASE ** (-jnp.arange(half, dtype=F32) * (2.0 / QK_ROPE))
    ang = pos.astype(F32)[:, None] * inv[None, :]
    cos, sin = jnp.cos(ang), jnp.sin(ang)
    cos_t = _pad_last(jnp.concatenate([cos, cos], axis=-1), ROPE_PAD)
    sin_t = _pad_last(jnp.concatenate([-sin, sin], axis=-1), ROPE_PAD)
    return jnp.tile(cos_t, (reps, 1)), jnp.tile(sin_t, (reps, 1))


def _swap_halves(w):
    half = QK_ROPE // 2
    return jnp.concatenate([w[..., half:], w[..., :half]], axis=-1)


def _pad_last(w, width):
    return jnp.pad(w, [(0, 0)] * (w.ndim - 1) + [(0, width - w.shape[-1])])


def _rope_weight(w_rope):
    return _pad_last(w_rope, ROPE_PAD).reshape(Q_LORA, N_HEADS * ROPE_PAD).astype(BF16)


def _pack_weights(w_in, mla_w_uq, mla_w_uk, mla_w_uv):
    b = [0]
    for w in (D_MODEL, D_MODEL, Q_LORA, KV_LORA, QK_ROPE, ATTN_WIDTH, D_MODEL, D_MODEL):
        b.append(b[-1] + w)
    w_kr = w_in[:, b[4]:b[5]]
    w_all = jnp.concatenate([w_in[:, :b[4]], _pad_last(w_kr, ROPE_PAD), _pad_last(_swap_halves(w_kr), ROPE_PAD),
                             w_in[:, b[5]:]], axis=-1).astype(BF16)
    w_nope = _pad_last(mla_w_uq[..., :QK_NOPE], LANE).reshape(Q_LORA, N_HEADS * LANE).astype(BF16)
    w_rope = mla_w_uq[..., QK_NOPE:]
    w_uk = jnp.pad(mla_w_uk.transpose(1, 2, 0), ((0, 0), (0, LANE - QK_NOPE), (0, 0))).astype(BF16)
    w_uv = mla_w_uv.transpose(1, 0, 2)
    lo = jnp.pad(w_uv, ((0, 0), (0, 0), (0, LANE - V_DIM)))
    hi = jnp.pad(w_uv, ((0, 0), (0, 0), (LANE - V_DIM, 0)))
    w_uv_pair = jnp.where((jnp.arange(N_HEADS) % 2 == 0)[:, None, None], lo, hi).astype(BF16)
    return dict(
        w_all=w_all, w_nope=w_nope, w_uk=w_uk, w_uv_pair=w_uv_pair,
        w_r=_rope_weight(w_rope), w_rs=_rope_weight(_swap_halves(w_rope)),
        w_uv_t=w_uv.transpose(0, 2, 1).astype(BF16))


def _layer(x, pos, h0, attend, w, batch, seq, tc, tm):
    n = x.shape[0]
    cos_t, sin_t = _rope_tables(pos, max(tm // seq, 1))
    row = lambda v: v.reshape(1, -1).astype(F32)
    u, gs, cq, ckv, kr, kf, vt, ga, ms, ma = _in_proj(x, row(w["norm_in"]), w["w_all"], row(w["mla_kv_norm"]),
                                                      cos_t, sin_t, tm)
    o = attend(cq, cos_t, sin_t, ckv, kr, kf, vt).reshape(n, ATTN_WIDTH)
    y_ssm, state = _s5_branch(u, h0, w["ssm"], batch, seq, tc)
    y = _merge(x, y_ssm, u, gs, o, ga, ms, ma, row(w["ssm_d"]), w["ssm_w_glu"].astype(BF16), row(w["ssm_b_glu"]),
               w["w_br_ssm"].astype(BF16), w["w_br_attn"].astype(BF16), w["w_out"].astype(BF16),
               row(w["norm_final"]), tm)
    return y, ckv, kr, state


def kernel(x_prompt, x_sample, cache_ckv, cache_krope, state_ssm, page_table,
           norm_in, w_in, ssm_lambda_re, ssm_lambda_im, ssm_log_dt, ssm_b_re, ssm_b_im,
           ssm_c_re, ssm_c_im, ssm_d, ssm_w_glu, ssm_b_glu, w_br_ssm,
           mla_q_norm, mla_w_uq, mla_kv_norm, mla_w_uk, mla_w_uv, w_br_attn, w_out, norm_final):
    b, l, _ = x_prompt.shape
    bd, t, _ = x_sample.shape
    w = dict(norm_in=norm_in, mla_kv_norm=mla_kv_norm,
             ssm=(ssm_lambda_re, ssm_lambda_im, ssm_log_dt, ssm_b_re, ssm_b_im, ssm_c_re, ssm_c_im),
             ssm_d=ssm_d, ssm_w_glu=ssm_w_glu, ssm_b_glu=ssm_b_glu, w_br_ssm=w_br_ssm,
             w_br_attn=w_br_attn, w_out=w_out, norm_final=norm_final,
             **_pack_weights(w_in, mla_w_uq, mla_w_uk, mla_w_uv))
    tm = 256
    gq = mla_q_norm.reshape(1, -1).astype(F32)

    def attend_prompt(cq, cos_t, sin_t, ckv, kr, kf, vt):
        qt = _q_prep_t(cq, gq, w["w_nope"].T, w["w_r"].T, w["w_rs"].T, w["w_uk"].transpose(0, 2, 1),
                       cos_t.T, sin_t.T, tm, b, l)
        return _prompt_attn(qt, kf.reshape(b, l, QK_WIDTH), vt, w["w_uv_t"], tm, cg=4096)

    def attend_sample(cq, cos_t, sin_t, ckv, kr, kf, vt):
        q = _q_prep(cq, gq, w["w_nope"], w["w_r"], w["w_rs"], w["w_uk"], cos_t, sin_t, tm, bd, t, F32)
        return _paged_attn(q.reshape(bd, N_HEADS * t, QK_WIDTH), ckv.reshape(bd, t, KV_LORA),
                           kr.reshape(bd, t, QK_ROPE), cache_ckv, cache_krope, page_table, w["w_uv_pair"],
                           pps=32)

    y_p, ckv_p, kr_p, ssm_p = _layer(x_prompt.reshape(b * l, D_MODEL), jnp.arange(l), None, attend_prompt,
                                     w, b, l, 16, tm)
    y_s, ckv_s, kr_s, ssm_s = _layer(x_sample.reshape(bd * t, D_MODEL), PAST_LEN + jnp.arange(t), state_ssm,
                                     attend_sample, w, bd, t, t, tm)
    return (y_p.reshape(b, l, D_MODEL), y_s.reshape(bd, t, D_MODEL),
            ckv_p.reshape(b, l, KV_LORA), kr_p.reshape(b, l, QK_ROPE), ssm_p,
            ckv_s.reshape(bd, t, KV_LORA), kr_s.reshape(bd, t, QK_ROPE), ssm_s)
```

```python
import functools
import math

import jax
import jax.numpy as jnp
from jax import lax
from jax.experimental import pallas as pl
from jax.experimental.pallas import tpu as pltpu

D_MODEL = 1024
PAST_LEN = 16384
SSM_GROUP = 16
SSM_GROUPS = 64
SSM_STATE = 64
N_HEADS = 16
QK_NOPE = 64
QK_ROPE = 32
V_DIM = 64
Q_LORA = 384
KV_LORA = 256
ATTN_WIDTH = N_HEADS * V_DIM
ROPE_BASE = 10000.0
ATTN_SCALE = (QK_NOPE + QK_ROPE) ** -0.5
NEG_INF = -1e30
NORM_EPS = 1e-6
LOG2_E = math.log2(math.e)

LANE = 128
ROPE_PAD = LANE
QK_WIDTH = KV_LORA + ROPE_PAD
VMEM_LIMIT = 56 * 1024 * 1024

F32 = jnp.float32
BF16 = jnp.bfloat16

_C_U = 0
_C_GS = _C_U + D_MODEL
_C_CQ = _C_GS + D_MODEL
_C_CKV = _C_CQ + Q_LORA
_C_KR = _C_CKV + KV_LORA
_C_KRS = _C_KR + ROPE_PAD
_C_GA = _C_KRS + ROPE_PAD
_C_MS = _C_GA + ATTN_WIDTH
_C_MA = _C_MS + D_MODEL
_C_END = _C_MA + D_MODEL


def _rms(x, g):
    return x * lax.rsqrt(jnp.mean(x * x, axis=-1, keepdims=True) + NORM_EPS) * g


def _dot(a, b):
    return jnp.dot(a, b, preferred_element_type=F32)


def _dot_nt(a, b):
    return lax.dot_general(a, b, (((1,), (1,)), ((), ())), preferred_element_type=F32)


def _params(*sem):
    return pltpu.CompilerParams(dimension_semantics=sem, vmem_limit_bytes=VMEM_LIMIT)


def _in_proj_kernel(x_ref, gin_ref, w_ref, gkv_ref, cos_ref, sin_ref,
                    u_ref, gs_ref, cq_ref, ckv_ref, kr_ref, kf_ref, vt_ref, ga_ref, ms_ref, ma_ref):
    xn = _rms(x_ref[...], gin_ref[...]).astype(BF16)

    def proj(lo, hi):
        return _dot(xn, w_ref[:, lo:hi])

    u_ref[...] = proj(_C_U, _C_GS)
    gs_ref[...] = proj(_C_GS, _C_CQ).astype(BF16)
    cq_ref[...] = proj(_C_CQ, _C_CKV)
    ckv = _rms(proj(_C_CKV, _C_KR), gkv_ref[...])
    ckv_ref[...] = ckv
    kr = proj(_C_KR, _C_KRS) * cos_ref[...] + proj(_C_KRS, _C_GA) * sin_ref[...]
    kr_ref[...] = kr[:, :QK_ROPE]
    kf_ref[:, :KV_LORA] = ckv.astype(BF16)
    kf_ref[:, KV_LORA:] = kr.astype(BF16)
    vt_ref[0] = ckv.T.astype(BF16)
    ga_ref[...] = proj(_C_GA, _C_MS).astype(BF16)
    ms_ref[...] = proj(_C_MS, _C_MA).astype(BF16)
    ma_ref[...] = proj(_C_MA, _C_END).astype(BF16)


def _in_proj(x, gin, w_all, gkv, cos_k, sin_k, tm):
    n = x.shape[0]
    nblk = cos_k.shape[0] // tm
    row = lambda w: pl.BlockSpec((tm, w), lambda i: (i, 0))
    full = lambda a: pl.BlockSpec(a.shape, lambda i: (0,) * a.ndim)
    tab = pl.BlockSpec((tm, ROPE_PAD), lambda i: (i % nblk, 0))
    widths = (D_MODEL, D_MODEL, Q_LORA, KV_LORA, QK_ROPE, QK_WIDTH, None, ATTN_WIDTH, D_MODEL, D_MODEL)
    dtypes = (F32, BF16, F32, F32, F32, BF16, BF16, BF16, BF16, BF16)
    vt_spec = pl.BlockSpec((1, KV_LORA, tm), lambda i: (i, 0, 0))
    shape = lambda w: (n, w) if w else (n // tm, KV_LORA, tm)
    return pl.pallas_call(
        _in_proj_kernel,
        grid=(n // tm,),
        in_specs=[row(D_MODEL), full(gin), full(w_all), full(gkv), tab, tab],
        out_specs=[row(w) if w else vt_spec for w in widths],
        out_shape=[jax.ShapeDtypeStruct(shape(w), d) for w, d in zip(widths, dtypes)],
        compiler_params=_params("parallel"),
        name="in_proj",
    )(x, gin, w_all, gkv, cos_k, sin_k)


def _q_prep_kernel(cq_ref, gq_ref, wn_ref, wr_ref, wrs_ref, wuk_ref, cos_ref, sin_ref, q_ref):
    nb, _, t, _ = q_ref.shape
    cqn = _rms(cq_ref[...], gq_ref[...]).astype(BF16)
    qn = _dot(cqn, wn_ref[...]).astype(BF16)
    qr = _dot(cqn, wr_ref[...])
    qrs = _dot(cqn, wrs_ref[...])
    cos = cos_ref[...]
    sin = sin_ref[...]
    for h in range(N_HEADS):
        sl = slice(h * LANE, (h + 1) * LANE)
        q_lat = _dot(qn[:, sl], wuk_ref[h])
        q_rope = qr[:, sl] * cos + qrs[:, sl] * sin
        q_ref[:, h, :, :KV_LORA] = q_lat.reshape(nb, t, KV_LORA).astype(q_ref.dtype)
        q_ref[:, h, :, KV_LORA:] = q_rope.reshape(nb, t, ROPE_PAD).astype(q_ref.dtype)


def _q_prep(cq, gq, wn, wr, wrs, wuk, cos_q, sin_q, tm, batch, seq, out_dtype):
    n = cq.shape[0]
    nblk = cos_q.shape[0] // tm
    full = lambda a: pl.BlockSpec(a.shape, lambda i: (0,) * a.ndim)
    tab = pl.BlockSpec((tm, ROPE_PAD), lambda i: (i % nblk, 0))
    if seq >= tm:
        per = seq // tm
        out_spec = pl.BlockSpec((1, N_HEADS, tm, QK_WIDTH), lambda i: (i // per, 0, i % per, 0))
    else:
        out_spec = pl.BlockSpec((tm // seq, N_HEADS, seq, QK_WIDTH), lambda i: (i, 0, 0, 0))
    return pl.pallas_call(
        _q_prep_kernel,
        grid=(n // tm,),
        in_specs=[pl.BlockSpec((tm, Q_LORA), lambda i: (i, 0)), full(gq), full(wn), full(wr), full(wrs),
                  full(wuk), tab, tab],
        out_specs=out_spec,
        out_shape=jax.ShapeDtypeStruct((batch, N_HEADS, seq, QK_WIDTH), out_dtype),
        compiler_params=_params("parallel"),
        name="q_prep",
    )(cq, gq, wn, wr, wrs, wuk, cos_q, sin_q)


def _q_prep_t_kernel(cq_ref, gq_ref, wn_ref, wr_ref, wrs_ref, wuk_ref, cos_ref, sin_ref, q_ref):
    tm = cq_ref.shape[0]
    cqn = _rms(cq_ref[...], gq_ref[...]).astype(BF16)
    qn = _dot_nt(wn_ref[...], cqn).astype(BF16)
    qr = _dot_nt(wr_ref[...], cqn)
    qrs = _dot_nt(wrs_ref[...], cqn)
    cos = cos_ref[...]
    sin = sin_ref[...]
    q_ref[0, 0, KV_LORA + QK_ROPE:, :] = jnp.zeros((ROPE_PAD - QK_ROPE, N_HEADS * tm), BF16)
    for h in range(N_HEADS):
        sl = slice(h * LANE, (h + 1) * LANE)
        rl = slice(h * QK_ROPE, (h + 1) * QK_ROPE)
        cl = slice(h * tm, (h + 1) * tm)
        q_ref[0, 0, :KV_LORA, cl] = _dot(wuk_ref[h], qn[sl]).astype(BF16)
        q_ref[0, 0, KV_LORA:KV_LORA + QK_ROPE, cl] = (qr[rl] * cos + qrs[rl] * sin).astype(BF16)


def _q_prep_t(cq, gq, wn_t, wr_t, wrs_t, wuk_t, cos_qt, sin_qt, tm, batch, seq):
    n = cq.shape[0]
    per = seq // tm
    full = lambda a: pl.BlockSpec(a.shape, lambda i: (0,) * a.ndim)
    tab = pl.BlockSpec((QK_ROPE, tm), lambda i: (0, i % per))
    return pl.pallas_call(
        _q_prep_t_kernel,
        grid=(n // tm,),
        in_specs=[pl.BlockSpec((tm, Q_LORA), lambda i: (i, 0)), full(gq), full(wn_t), full(wr_t), full(wrs_t),
                  full(wuk_t), tab, tab],
        out_specs=pl.BlockSpec((1, 1, QK_WIDTH, N_HEADS * tm), lambda i: (i // per, i % per, 0, 0)),
        out_shape=jax.ShapeDtypeStruct((batch, per, QK_WIDTH, N_HEADS * tm), BF16),
        compiler_params=_params("parallel"),
        name="q_prep_t",
    )(cq, gq, wn_t, wr_t, wrs_t, wuk_t, cos_qt, sin_qt)


def _softmax_step(s, v, m_ref, l_ref, acc_ref):
    m_prev = m_ref[...]
    m_new = jnp.maximum(m_prev, jnp.max(s, axis=-1, keepdims=True))
    alpha = jnp.exp(m_prev - m_new)
    p = jnp.exp(s - m_new)
    l_ref[...] = alpha * l_ref[...] + jnp.sum(p, axis=-1, keepdims=True)
    acc_ref[...] = alpha * acc_ref[...] + _dot(p.astype(BF16), v)
    m_ref[...] = m_new


def _value_up(o_lat, wuv_ref, t):
    cols = []
    for j in range(N_HEADS // 2):
        a = o_lat[(2 * j) * t:(2 * j + 1) * t].astype(BF16)
        b = o_lat[(2 * j + 1) * t:(2 * j + 2) * t].astype(BF16)
        cols.append(_dot(a, wuv_ref[2 * j]) + _dot(b, wuv_ref[2 * j + 1]))
    return jnp.concatenate(cols, axis=-1)


def _init_softmax(m_ref, l_ref, acc_ref):
    m_ref[...] = jnp.full(m_ref.shape, -jnp.inf, F32)
    l_ref[...] = jnp.zeros(l_ref.shape, F32)
    acc_ref[...] = jnp.zeros(acc_ref.shape, F32)


def _prompt_attn_kernel(qt_ref, k_ref, vt_ref, wuvt_ref, o_ref, m_ref, l_ref, acc_ref, *, tq, cg):
    qi = pl.program_id(1)
    cols = N_HEADS * tq
    _init_softmax(m_ref, l_ref, acc_ref)

    def chunk(kb, diagonal):
        k = k_ref[0, pl.ds(pl.multiple_of(kb * tq, tq), tq), :]
        vt = vt_ref[kb]
        if diagonal:
            key = lax.broadcasted_iota(jnp.int32, (tq, cg), 0)
            tok = lax.broadcasted_iota(jnp.int32, (tq, cg), 1) & (tq - 1)
            causal = key <= tok
        for g in range(cols // cg):
            sl = slice(g * cg, (g + 1) * cg)
            s = _dot(k, qt_ref[0, 0, :, sl]) * (ATTN_SCALE * LOG2_E)
            if diagonal:
                s = jnp.where(causal, s, NEG_INF)
            m_prev = m_ref[:, sl]
            m_new = jnp.maximum(m_prev, jnp.max(s, axis=0, keepdims=True))
            alpha = jnp.exp2(m_prev - m_new)
            p = jnp.exp2(s - m_new)
            l_ref[:, sl] = alpha * l_ref[:, sl] + jnp.sum(p, axis=0, keepdims=True)
            acc_ref[:, sl] = alpha * acc_ref[:, sl] + _dot(vt, p.astype(BF16))
            m_ref[:, sl] = m_new

    def body(kb, carry):
        chunk(kb, False)
        return carry

    lax.fori_loop(0, qi, body, 0)
    chunk(qi, True)
    heads = []
    for h in range(N_HEADS):
        sl = slice(h * tq, (h + 1) * tq)
        o_lat = (acc_ref[:, sl] / l_ref[:, sl]).astype(BF16)
        heads.append(_dot(wuvt_ref[h], o_lat))
    o_ref[0] = jnp.concatenate(heads, axis=0).T


def _prompt_attn(qt, kf, vt, wuv_t, tq, cg):
    b, nq, _, cols = qt.shape
    l = nq * tq
    return pl.pallas_call(
        functools.partial(_prompt_attn_kernel, tq=tq, cg=cg),
        grid=(b, nq),
        in_specs=[pl.BlockSpec((1, 1, QK_WIDTH, cols), lambda bi, qi: (bi, qi, 0, 0)),
                  pl.BlockSpec((1, l, QK_WIDTH), lambda bi, qi: (bi, 0, 0)),
                  pl.BlockSpec((nq, KV_LORA, tq), lambda bi, qi: (bi, 0, 0)),
                  pl.BlockSpec(wuv_t.shape, lambda bi, qi: (0, 0, 0))],
        out_specs=pl.BlockSpec((1, tq, ATTN_WIDTH), lambda bi, qi: (bi, qi, 0)),
        out_shape=jax.ShapeDtypeStruct((b, l, ATTN_WIDTH), F32),
        scratch_shapes=[pltpu.VMEM((1, cols), F32), pltpu.VMEM((1, cols), F32),
                        pltpu.VMEM((KV_LORA, cols), F32)],
        compiler_params=_params("parallel", "arbitrary"),
        name="prompt_attn",
    )(qt, kf, vt, wuv_t)


def _paged_attn_kernel(pt_ref, q_ref, ks_ref, rs_ref, wuv_ref, ckv_hbm, kr_hbm, o_ref,
                       m_ref, l_ref, acc_ref, kbuf, rbuf, sem, *, pps, page, t):
    b = pl.program_id(0)
    j = pl.program_id(1)
    nj = pl.num_programs(1)
    step = b * nj + j
    slot = step % 2
    rows = N_HEADS * t

    def page_copies(bb, jj, sl):
        for i in range(pps):
            pg = pt_ref[bb, jj * pps + i]
            yield pltpu.make_async_copy(ckv_hbm.at[pg], kbuf.at[sl, i], sem.at[0, sl])
            yield pltpu.make_async_copy(kr_hbm.at[pg], rbuf.at[sl, i], sem.at[1, sl])

    @pl.when(step == 0)
    def _():
        for cp in page_copies(b, j, slot):
            cp.start()

    @pl.when(step + 1 < pl.num_programs(0) * nj)
    def _():
        last = j == nj - 1
        for cp in page_copies(jnp.where(last, b + 1, b), jnp.where(last, 0, j + 1), 1 - slot):
            cp.start()

    q = q_ref[0].astype(BF16)
    q_lat = q[:, :KV_LORA]
    q_rope = q[:, KV_LORA:KV_LORA + QK_ROPE]

    @pl.when(j == 0)
    def _():
        _init_softmax(m_ref, l_ref, acc_ref)
        k = jnp.concatenate([ks_ref[0], jnp.zeros((page - t, KV_LORA), F32)], axis=0).astype(BF16)
        r = jnp.concatenate([rs_ref[0], jnp.zeros((page - t, QK_ROPE), F32)], axis=0).astype(BF16)
        s = (_dot_nt(q_lat, k) + _dot_nt(q_rope, r)) * ATTN_SCALE
        tok = lax.broadcasted_iota(jnp.int32, (t, page), 0)
        key = lax.broadcasted_iota(jnp.int32, (t, page), 1)
        s = jnp.where((key <= tok)[None], s.reshape(N_HEADS, t, page), NEG_INF).reshape(rows, page)
        _softmax_step(s, k, m_ref, l_ref, acc_ref)

    for cp in page_copies(b, j, slot):
        cp.wait()
    k = kbuf[slot].reshape(pps * page, KV_LORA).astype(BF16)
    r_t = jnp.concatenate([rbuf[slot, i] for i in range(pps)], axis=1).astype(BF16)
    s = (_dot_nt(q_lat, k) + _dot(q_rope, r_t)) * ATTN_SCALE
    _softmax_step(s, k, m_ref, l_ref, acc_ref)

    @pl.when(j == nj - 1)
    def _():
        o_ref[0] = _value_up(acc_ref[...] / l_ref[...], wuv_ref, t)


def _paged_attn(q, ckv_new, kr_new, cache_ckv, cache_kr, page_table, wuv, pps):
    bd, rows, _ = q.shape
    t = rows // N_HEADS
    _, page, _ = cache_ckv.shape
    n_pages = page_table.shape[1]
    assert n_pages % pps == 0 and t <= page
    grid_spec = pltpu.PrefetchScalarGridSpec(
        num_scalar_prefetch=1,
        grid=(bd, n_pages // pps),
        in_specs=[pl.BlockSpec((1, rows, QK_WIDTH), lambda b, j, pt: (b, 0, 0)),
                  pl.BlockSpec((1, t, KV_LORA), lambda b, j, pt: (b, 0, 0)),
                  pl.BlockSpec((1, t, QK_ROPE), lambda b, j, pt: (b, 0, 0)),
                  pl.BlockSpec(wuv.shape, lambda b, j, pt: (0, 0, 0)),
                  pl.BlockSpec(memory_space=pl.ANY), pl.BlockSpec(memory_space=pl.ANY)],
        out_specs=pl.BlockSpec((1, t, ATTN_WIDTH), lambda b, j, pt: (b, 0, 0)),
        scratch_shapes=[pltpu.VMEM((rows, 1), F32), pltpu.VMEM((rows, 1), F32),
                        pltpu.VMEM((rows, KV_LORA), F32),
                        pltpu.VMEM((2, pps, page, KV_LORA), F32), pltpu.VMEM((2, pps, QK_ROPE, page), F32),
                        pltpu.SemaphoreType.DMA((2, 2))],
    )
    return pl.pallas_call(
        functools.partial(_paged_attn_kernel, pps=pps, page=page, t=t),
        grid_spec=grid_spec,
        out_shape=jax.ShapeDtypeStruct((bd, t, ATTN_WIDTH), F32),
        compiler_params=_params("arbitrary", "arbitrary"),
        name="paged_attn",
    )(page_table, q, ckv_new, kr_new, wuv, cache_ckv, cache_kr.transpose(0, 2, 1))


SET_GROUPS = LANE // SSM_GROUP
N_SETS = SSM_GROUPS // SET_GROUPS
SET_STATE = SET_GROUPS * SSM_STATE


def _cmul(xr, xi, ar, ai):
    return xr * ar - xi * ai, xr * ai + xi * ar


def _s5_kernel(*refs, tc, nc, has_h0):
    if has_h0:
        u_ref, d_ref, ws_ref, bz_ref, cz_ref, pw_ref, h0_ref, y_ref, hl_ref = refs
    else:
        u_ref, d_ref, ws_ref, bz_ref, cz_ref, pw_ref, y_ref, hl_ref = refs
    rows = u_ref.shape[0] // tc
    u_step = [u_ref[pl.ds(s, rows, stride=tc), :] for s in range(tc)]
    u_cat = jnp.concatenate([u.astype(BF16) for u in u_step], axis=1)
    z = _dot(u_cat, bz_ref[0])
    sr, si = z[:, :SET_STATE], z[:, SET_STATE:]
    if has_h0:
        assert nc == 1
        pr, pi = h0_ref[0, 0, :, :SET_STATE], h0_ref[0, 0, :, SET_STATE:]
        dr, di = _cmul(pr, pi, pw_ref[0, 0, 0:1], pw_ref[0, 0, 1:2])
        sr, si = sr + dr, si + di
    else:
        chunk = lax.broadcasted_iota(jnp.int32, (rows, SET_STATE), 0) & (nc - 1)
        for k in range(nc.bit_length() - 1):
            sh = 1 << k
            keep = chunk >= sh
            tr = jnp.where(keep, pltpu.roll(sr, sh, axis=0), 0.0)
            ti = jnp.where(keep, pltpu.roll(si, sh, axis=0), 0.0)
            dr, di = _cmul(tr, ti, pw_ref[0, k, 0:1], pw_ref[0, k, 1:2])
            sr, si = sr + dr, si + di
        pr = jnp.where(chunk >= 1, pltpu.roll(sr, 1, axis=0), 0.0)
        pi = jnp.where(chunk >= 1, pltpu.roll(si, 1, axis=0), 0.0)
    y_state = _dot(jnp.concatenate([pr, pi], axis=1).astype(BF16), cz_ref[0])
    for t in range(tc):
        y_t = y_state[:, t * LANE:(t + 1) * LANE] + _dot(u_cat[:, :(t + 1) * LANE], ws_ref[0, (tc - 1 - t) * LANE:, :])
        y_ref[pl.ds(t, rows, stride=tc), :] = y_t + d_ref[...] * u_step[t]
    s_last = jnp.concatenate([sr, si], axis=1)
    if nc == 1:
        hl_ref[0, 0] = s_last
    else:
        for q in range(rows // nc):
            hl_ref[0, 0, q:q + 1, :] = s_last[q * nc + nc - 1:q * nc + nc, :]


def _s5(u, d, ws, bz, cz, pw, h0, tc, nc, block_rows):
    n = u.shape[0]
    nblk = n // block_rows
    nseq = block_rows // (tc * nc)
    per_set = lambda a: pl.BlockSpec((1,) + a.shape[1:], lambda si, rb: (si,) + (0,) * (a.ndim - 1))
    tok = pl.BlockSpec((block_rows, LANE), lambda si, rb: (rb, si))
    state = pl.BlockSpec((1, 1, nseq, 2 * SET_STATE), lambda si, rb: (si, rb, 0, 0))
    skip = pl.BlockSpec((1, LANE), lambda si, rb: (0, si))
    ins = [u, d, ws, bz, cz, pw] + ([h0] if h0 is not None else [])
    return pl.pallas_call(
        functools.partial(_s5_kernel, tc=tc, nc=nc, has_h0=h0 is not None),
        grid=(N_SETS, nblk),
        in_specs=[tok, skip, per_set(ws), per_set(bz), per_set(cz), per_set(pw)]
                 + ([state] if h0 is not None else []),
        out_specs=[tok, state],
        out_shape=[jax.ShapeDtypeStruct((n, SSM_GROUPS * SSM_GROUP), F32),
                   jax.ShapeDtypeStruct((N_SETS, nblk, nseq, 2 * SET_STATE), F32)],
        compiler_params=_params("parallel", "parallel"),
        name="s5",
    )(*ins)


def _block_diag(x):
    a, b = x.shape[-2:]
    rows = x.reshape(x.shape[:-3] + (SET_GROUPS * a, b)).astype(BF16)
    spread = lax.broadcasted_iota(jnp.int32, (b, SET_GROUPS * b), 1) % b == lax.broadcasted_iota(
        jnp.int32, (b, SET_GROUPS * b), 0)
    wide = jnp.dot(rows, spread.astype(BF16), preferred_element_type=BF16)
    shape = (SET_GROUPS * a, SET_GROUPS * b)
    same = lax.broadcasted_iota(jnp.int32, shape, 0) // a == lax.broadcasted_iota(jnp.int32, shape, 1) // b
    return jnp.where(same, wide, jnp.zeros((), BF16))


def _s5_weights(lam_re, lam_im, log_dt, b_re, b_im, c_re, c_im, tc, n_levels):
    hi = lax.Precision.HIGHEST
    lam = lax.complex(lam_re.astype(F32), lam_im.astype(F32))
    ldt = lam * jnp.exp(log_dt.astype(F32))[:, None]
    lam_bar = jnp.exp(ldt)
    b_bar = ((lam_bar - 1.0) / lam)[..., None] * lax.complex(b_re.astype(F32), b_im.astype(F32))
    c = lax.complex(c_re.astype(F32), c_im.astype(F32))
    sets = lambda x: x.reshape(x.shape[:-3] + (N_SETS, SET_GROUPS) + x.shape[-2:])
    steps = jnp.arange(tc + 1, dtype=F32)
    pw = jnp.exp(ldt[None] * steps[:, None, None])
    kern = jnp.real(jnp.einsum('gop,dgp,gpi->dgio', c, pw[:tc], b_bar, precision=hi))
    ws = _block_diag(sets(kern[::-1])).transpose(1, 0, 2, 3).reshape(N_SETS, tc * LANE, LANE)
    bzc = (pw[tc - 1 - jnp.arange(tc)][..., None] * b_bar[None]).transpose(0, 1, 3, 2)
    bz = jnp.concatenate([_block_diag(sets(jnp.real(bzc))), _block_diag(sets(jnp.imag(bzc)))], axis=-1)
    bz = bz.transpose(1, 0, 2, 3).reshape(N_SETS, tc * LANE, 2 * SET_STATE)
    czc = (c[None] * pw[1:tc + 1][:, :, None, :]).transpose(0, 1, 3, 2)
    cz = jnp.concatenate([_block_diag(sets(jnp.real(czc))), _block_diag(sets(-jnp.imag(czc)))], axis=-2)
    cz = jnp.concatenate([cz[t] for t in range(tc)], axis=-1)
    lev = jnp.exp(ldt[None] * (tc * 2.0 ** jnp.arange(n_levels, dtype=F32))[:, None, None])
    pwl = jnp.stack([jnp.real(lev), jnp.imag(lev)], axis=1).reshape(n_levels, 2, N_SETS, SET_STATE)
    return ws, bz, cz, pwl.transpose(2, 0, 1, 3)


def _s5_branch(u, h0, ssm_w, d, batch, seq, tc):
    nc = seq // tc
    n_levels = max(nc.bit_length() - 1, 1)
    assert nc & (nc - 1) == 0 and (h0 is None or nc == 1)
    ws, bz, cz, pwl = _s5_weights(*ssm_w, tc, n_levels)
    block_rows = seq if nc > 1 else batch * seq
    nseq = block_rows // seq
    h0_t = None
    if h0 is not None:
        h0_t = h0.astype(F32).reshape(batch // nseq, nseq, N_SETS, SET_GROUPS, SSM_STATE, 2)
        h0_t = h0_t.transpose(2, 0, 1, 5, 3, 4).reshape(N_SETS, batch // nseq, nseq, 2 * SET_STATE)
    y, hl = _s5(u, d, ws, bz, cz, pwl, h0_t, tc, nc, block_rows)
    hl = hl.reshape(N_SETS, batch, 2, SET_GROUPS, SSM_STATE).transpose(1, 0, 3, 4, 2)
    return y, hl.reshape(batch, SSM_GROUPS, SSM_STATE, 2)


def _merge_kernel(x_ref, y_ref, gs_ref, o_ref, ga_ref, ms_ref, ma_ref,
                  wglu_ref, bglu_ref, wbs_ref, wba_ref, wout_ref, gf_ref, out_ref):
    gate = lambda ref: ref[...].astype(F32)
    zg = jax.nn.gelu(y_ref[...])
    glu = zg * jax.nn.sigmoid(_dot(zg.astype(BF16), wglu_ref[...]) + bglu_ref[...])
    y_s = _dot((glu * jax.nn.silu(gate(gs_ref))).astype(BF16), wbs_ref[...])
    y_a = _dot((o_ref[...] * jax.nn.silu(gate(ga_ref))).astype(BF16), wba_ref[...])
    merged = jax.nn.sigmoid(gate(ms_ref)) * y_s + jax.nn.sigmoid(gate(ma_ref)) * y_a
    h = x_ref[...] + _dot(merged.astype(BF16), wout_ref[...])
    out_ref[...] = _rms(h, gf_ref[...])


def _merge(x, y, gs, o, ga, ms, ma, wglu, bglu, wbs, wba, wout, gf, tm):
    n = x.shape[0]
    row = pl.BlockSpec((tm, D_MODEL), lambda i: (i, 0))
    full = lambda a: pl.BlockSpec(a.shape, lambda i: (0,) * a.ndim)
    consts = (wglu, bglu, wbs, wba, wout, gf)
    return pl.pallas_call(
        _merge_kernel,
        grid=(n // tm,),
        in_specs=[row] * 7 + [full(a) for a in consts],
        out_specs=row,
        out_shape=jax.ShapeDtypeStruct((n, D_MODEL), F32),
        compiler_params=_params("parallel"),
        name="merge",
    )(x, y, gs, o, ga, ms, ma, *consts)


def _rope_tables(pos, reps):
    half = QK_ROPE // 2
    inv = ROPE_BASE ** (-jnp.arange(half, dtype=F32) * (2.0 / QK_ROPE))
    ang = pos.astype(F32)[:, None] * inv[None, :]
    cos, sin = jnp.cos(ang), jnp.sin(ang)
    cos_t = _pad_last(jnp.concatenate([cos, cos], axis=-1), ROPE_PAD)
    sin_t = _pad_last(jnp.concatenate([-sin, sin], axis=-1), ROPE_PAD)
    return jnp.tile(cos_t, (reps, 1)), jnp.tile(sin_t, (reps, 1))


def _swap_halves(w):
    half = QK_ROPE // 2
    return jnp.concatenate([w[..., half:], w[..., :half]], axis=-1)


def _pad_last(w, width):
    return jnp.pad(w, [(0, 0)] * (w.ndim - 1) + [(0, width - w.shape[-1])])


def _rope_weight(w_rope):
    return _pad_last(w_rope, ROPE_PAD).reshape(Q_LORA, N_HEADS * ROPE_PAD).astype(BF16)


def _rope_weight_t(w_rope):
    return w_rope.transpose(1, 2, 0).reshape(N_HEADS * QK_ROPE, Q_LORA).astype(BF16)


def _pack_weights(w_in, mla_w_uq, mla_w_uk, mla_w_uv):
    b = [0]
    for w in (D_MODEL, D_MODEL, Q_LORA, KV_LORA, QK_ROPE, ATTN_WIDTH, D_MODEL, D_MODEL):
        b.append(b[-1] + w)
    w_kr = w_in[:, b[4]:b[5]]
    w_all = jnp.concatenate([w_in[:, :b[4]], _pad_last(w_kr, ROPE_PAD), _pad_last(_swap_halves(w_kr), ROPE_PAD),
                             w_in[:, b[5]:]], axis=-1).astype(BF16)
    w_nope = _pad_last(mla_w_uq[..., :QK_NOPE], LANE).reshape(Q_LORA, N_HEADS * LANE).astype(BF16)
    w_rope = mla_w_uq[..., QK_NOPE:]
    w_uk = jnp.pad(mla_w_uk.transpose(1, 2, 0), ((0, 0), (0, LANE - QK_NOPE), (0, 0))).astype(BF16)
    w_uv = mla_w_uv.transpose(1, 0, 2)
    lo = jnp.pad(w_uv, ((0, 0), (0, 0), (0, LANE - V_DIM)))
    hi = jnp.pad(w_uv, ((0, 0), (0, 0), (LANE - V_DIM, 0)))
    w_uv_pair = jnp.where((jnp.arange(N_HEADS) % 2 == 0)[:, None, None], lo, hi).astype(BF16)
    return dict(
        w_all=w_all, w_nope=w_nope, w_uk=w_uk, w_uv_pair=w_uv_pair,
        w_r=_rope_weight(w_rope), w_rs=_rope_weight(_swap_halves(w_rope)),
        w_r_t=_rope_weight_t(w_rope), w_rs_t=_rope_weight_t(_swap_halves(w_rope)),
        w_uv_t=w_uv.transpose(0, 2, 1).astype(BF16))


def _layer(x, pos, h0, attend, w, batch, seq, tc, tm):
    n = x.shape[0]
    cos_t, sin_t = _rope_tables(pos, max(tm // seq, 1))
    row = lambda v: v.reshape(1, -1).astype(F32)
    u, gs, cq, ckv, kr, kf, vt, ga, ms, ma = _in_proj(x, row(w["norm_in"]), w["w_all"], row(w["mla_kv_norm"]),
                                                      cos_t, sin_t, tm)
    o = attend(cq, cos_t, sin_t, ckv, kr, kf, vt).reshape(n, ATTN_WIDTH)
    y_ssm, state = _s5_branch(u, h0, w["ssm"], row(w["ssm_d"]), batch, seq, tc)
    y = _merge(x, y_ssm, gs, o, ga, ms, ma, w["ssm_w_glu"].astype(BF16), row(w["ssm_b_glu"]),
               w["w_br_ssm"].astype(BF16), w["w_br_attn"].astype(BF16), w["w_out"].astype(BF16),
               row(w["norm_final"]), tm)
    return y, ckv, kr, state


def kernel(x_prompt, x_sample, cache_ckv, cache_krope, state_ssm, page_table,
           norm_in, w_in, ssm_lambda_re, ssm_lambda_im, ssm_log_dt, ssm_b_re, ssm_b_im,
           ssm_c_re, ssm_c_im, ssm_d, ssm_w_glu, ssm_b_glu, w_br_ssm,
           mla_q_norm, mla_w_uq, mla_kv_norm, mla_w_uk, mla_w_uv, w_br_attn, w_out, norm_final):
    b, l, _ = x_prompt.shape
    bd, t, _ = x_sample.shape
    w = dict(norm_in=norm_in, mla_kv_norm=mla_kv_norm,
             ssm=(ssm_lambda_re, ssm_lambda_im, ssm_log_dt, ssm_b_re, ssm_b_im, ssm_c_re, ssm_c_im),
             ssm_d=ssm_d, ssm_w_glu=ssm_w_glu, ssm_b_glu=ssm_b_glu, w_br_ssm=w_br_ssm,
             w_br_attn=w_br_attn, w_out=w_out, norm_final=norm_final,
             **_pack_weights(w_in, mla_w_uq, mla_w_uk, mla_w_uv))
    tm = 256
    gq = mla_q_norm.reshape(1, -1).astype(F32)

    def attend_prompt(cq, cos_t, sin_t, ckv, kr, kf, vt):
        qt = _q_prep_t(cq, gq, w["w_nope"].T, w["w_r_t"], w["w_rs_t"], w["w_uk"].transpose(0, 2, 1),
                       cos_t.T[:QK_ROPE], sin_t.T[:QK_ROPE], tm, b, l)
        return _prompt_attn(qt, kf.reshape(b, l, QK_WIDTH), vt, w["w_uv_t"], tm, cg=4096)

    def attend_sample(cq, cos_t, sin_t, ckv, kr, kf, vt):
        q = _q_prep(cq, gq, w["w_nope"], w["w_r"], w["w_rs"], w["w_uk"], cos_t, sin_t, tm, bd, t, F32)
        return _paged_attn(q.reshape(bd, N_HEADS * t, QK_WIDTH), ckv.reshape(bd, t, KV_LORA),
                           kr.reshape(bd, t, QK_ROPE), cache_ckv, cache_krope, page_table, w["w_uv_pair"],
                           pps=32)

    y_p, ckv_p, kr_p, ssm_p = _layer(x_prompt.reshape(b * l, D_MODEL), jnp.arange(l), None, attend_prompt,
                                     w, b, l, 16, tm)
    y_s, ckv_s, kr_s, ssm_s = _layer(x_sample.reshape(bd * t, D_MODEL), PAST_LEN + jnp.arange(t), state_ssm,
                                     attend_sample, w, bd, t, t, tm)
    return (y_p.reshape(b, l, D_MODEL), y_s.reshape(bd, t, D_MODEL),
            ckv_p.reshape(b, l, KV_LORA), kr_p.reshape(b, l, QK_ROPE), ssm_p,
            ckv_s.reshape(bd, t, KV_LORA), kr_s.reshape(bd, t, QK_ROPE), ssm_s)
```

```python
import functools
import math

import jax
import jax.numpy as jnp
from jax import lax
from jax.experimental import pallas as pl
from jax.experimental.pallas import tpu as pltpu

D_MODEL = 1024
PAST_LEN = 16384
SSM_GROUP = 16
SSM_GROUPS = 64
SSM_STATE = 64
N_HEADS = 16
QK_NOPE = 64
QK_ROPE = 32
V_DIM = 64
Q_LORA = 384
KV_LORA = 256
ATTN_WIDTH = N_HEADS * V_DIM
ROPE_BASE = 10000.0
ATTN_SCALE = (QK_NOPE + QK_ROPE) ** -0.5
NEG_INF = -1e30
NORM_EPS = 1e-6
LOG2_E = math.log2(math.e)

LANE = 128
ROPE_PAD = LANE
QK_WIDTH = KV_LORA + ROPE_PAD
VMEM_LIMIT = 56 * 1024 * 1024

F32 = jnp.float32
BF16 = jnp.bfloat16

_C_U = 0
_C_GS = _C_U + D_MODEL
_C_CQ = _C_GS + D_MODEL
_C_CKV = _C_CQ + Q_LORA
_C_KR = _C_CKV + KV_LORA
_C_KRS = _C_KR + ROPE_PAD
_C_GA = _C_KRS + ROPE_PAD
_C_MS = _C_GA + ATTN_WIDTH
_C_MA = _C_MS + D_MODEL
_C_END = _C_MA + D_MODEL


def _rms(x, g):
    return x * lax.rsqrt(jnp.mean(x * x, axis=-1, keepdims=True) + NORM_EPS) * g


def _dot(a, b):
    return jnp.dot(a, b, preferred_element_type=F32)


def _dot_nt(a, b):
    return lax.dot_general(a, b, (((1,), (1,)), ((), ())), preferred_element_type=F32)


def _params(*sem):
    return pltpu.CompilerParams(dimension_semantics=sem, vmem_limit_bytes=VMEM_LIMIT)


def _in_proj_kernel(x_ref, gin_ref, w_ref, gkv_ref, cos_ref, sin_ref,
                    u_ref, gs_ref, cq_ref, ckv_ref, kr_ref, kf_ref, vt_ref, ga_ref, ms_ref, ma_ref):
    xn = _rms(x_ref[...], gin_ref[...]).astype(BF16)

    def proj(lo, hi):
        return _dot(xn, w_ref[:, lo:hi])

    u_ref[...] = proj(_C_U, _C_GS)
    gs_ref[...] = proj(_C_GS, _C_CQ).astype(BF16)
    cq_ref[...] = proj(_C_CQ, _C_CKV)
    ckv = _rms(proj(_C_CKV, _C_KR), gkv_ref[...])
    ckv_ref[...] = ckv
    kr = proj(_C_KR, _C_KRS) * cos_ref[...] + proj(_C_KRS, _C_GA) * sin_ref[...]
    kr_ref[...] = kr[:, :QK_ROPE]
    kf_ref[:, :KV_LORA] = ckv.astype(BF16)
    kf_ref[:, KV_LORA:] = kr.astype(BF16)
    vt_ref[0] = ckv.T.astype(BF16)
    ga_ref[...] = proj(_C_GA, _C_MS).astype(BF16)
    ms_ref[...] = proj(_C_MS, _C_MA).astype(BF16)
    ma_ref[...] = proj(_C_MA, _C_END).astype(BF16)


def _in_proj(x, gin, w_all, gkv, cos_k, sin_k, tm):
    n = x.shape[0]
    nblk = cos_k.shape[0] // tm
    row = lambda w: pl.BlockSpec((tm, w), lambda i: (i, 0))
    full = lambda a: pl.BlockSpec(a.shape, lambda i: (0,) * a.ndim)
    tab = pl.BlockSpec((tm, ROPE_PAD), lambda i: (i % nblk, 0))
    widths = (D_MODEL, D_MODEL, Q_LORA, KV_LORA, QK_ROPE, QK_WIDTH, None, ATTN_WIDTH, D_MODEL, D_MODEL)
    dtypes = (F32, BF16, F32, F32, F32, BF16, BF16, BF16, BF16, BF16)
    vt_spec = pl.BlockSpec((1, KV_LORA, tm), lambda i: (i, 0, 0))
    shape = lambda w: (n, w) if w else (n // tm, KV_LORA, tm)
    return pl.pallas_call(
        _in_proj_kernel,
        grid=(n // tm,),
        in_specs=[row(D_MODEL), full(gin), full(w_all), full(gkv), tab, tab],
        out_specs=[row(w) if w else vt_spec for w in widths],
        out_shape=[jax.ShapeDtypeStruct(shape(w), d) for w, d in zip(widths, dtypes)],
        compiler_params=_params("parallel"),
        name="in_proj",
    )(x, gin, w_all, gkv, cos_k, sin_k)


def _q_prep_kernel(cq_ref, gq_ref, wn_ref, wr_ref, wrs_ref, wuk_ref, cos_ref, sin_ref, q_ref):
    nb, _, t, _ = q_ref.shape
    cqn = _rms(cq_ref[...], gq_ref[...]).astype(BF16)
    qn = _dot(cqn, wn_ref[...]).astype(BF16)
    qr = _dot(cqn, wr_ref[...])
    qrs = _dot(cqn, wrs_ref[...])
    cos = cos_ref[...]
    sin = sin_ref[...]
    for h in range(N_HEADS):
        sl = slice(h * LANE, (h + 1) * LANE)
        q_lat = _dot(qn[:, sl], wuk_ref[h])
        q_rope = qr[:, sl] * cos + qrs[:, sl] * sin
        q_ref[:, h, :, :KV_LORA] = q_lat.reshape(nb, t, KV_LORA).astype(q_ref.dtype)
        q_ref[:, h, :, KV_LORA:] = q_rope.reshape(nb, t, ROPE_PAD).astype(q_ref.dtype)


def _q_prep(cq, gq, wn, wr, wrs, wuk, cos_q, sin_q, tm, batch, seq, out_dtype):
    n = cq.shape[0]
    nblk = cos_q.shape[0] // tm
    full = lambda a: pl.BlockSpec(a.shape, lambda i: (0,) * a.ndim)
    tab = pl.BlockSpec((tm, ROPE_PAD), lambda i: (i % nblk, 0))
    if seq >= tm:
        per = seq // tm
        out_spec = pl.BlockSpec((1, N_HEADS, tm, QK_WIDTH), lambda i: (i // per, 0, i % per, 0))
    else:
        out_spec = pl.BlockSpec((tm // seq, N_HEADS, seq, QK_WIDTH), lambda i: (i, 0, 0, 0))
    return pl.pallas_call(
        _q_prep_kernel,
        grid=(n // tm,),
        in_specs=[pl.BlockSpec((tm, Q_LORA), lambda i: (i, 0)), full(gq), full(wn), full(wr), full(wrs),
                  full(wuk), tab, tab],
        out_specs=out_spec,
        out_shape=jax.ShapeDtypeStruct((batch, N_HEADS, seq, QK_WIDTH), out_dtype),
        compiler_params=_params("parallel"),
        name="q_prep",
    )(cq, gq, wn, wr, wrs, wuk, cos_q, sin_q)


def _q_prep_t_kernel(cq_ref, gq_ref, wn_ref, wr_ref, wrs_ref, wuk_ref, cos_ref, sin_ref, q_ref):
    tm = cq_ref.shape[0]
    cqn = _rms(cq_ref[...], gq_ref[...]).astype(BF16)
    qn = _dot_nt(wn_ref[...], cqn).astype(BF16)
    qr = _dot_nt(wr_ref[...], cqn)
    qrs = _dot_nt(wrs_ref[...], cqn)
    cos = cos_ref[...]
    sin = sin_ref[...]
    q_ref[0, 0, KV_LORA + QK_ROPE:, :] = jnp.zeros((ROPE_PAD - QK_ROPE, N_HEADS * tm), BF16)
    for h in range(N_HEADS):
        sl = slice(h * LANE, (h + 1) * LANE)
        rl = slice(h * QK_ROPE, (h + 1) * QK_ROPE)
        cl = slice(h * tm, (h + 1) * tm)
        q_ref[0, 0, :KV_LORA, cl] = _dot(wuk_ref[h], qn[sl]).astype(BF16)
        q_ref[0, 0, KV_LORA:KV_LORA + QK_ROPE, cl] = (qr[rl] * cos + qrs[rl] * sin).astype(BF16)


def _q_prep_t(cq, gq, wn_t, wr_t, wrs_t, wuk_t, cos_qt, sin_qt, tm, batch, seq):
    n = cq.shape[0]
    per = seq // tm
    full = lambda a: pl.BlockSpec(a.shape, lambda i: (0,) * a.ndim)
    tab = pl.BlockSpec((QK_ROPE, tm), lambda i: (0, i % per))
    return pl.pallas_call(
        _q_prep_t_kernel,
        grid=(n // tm,),
        in_specs=[pl.BlockSpec((tm, Q_LORA), lambda i: (i, 0)), full(gq), full(wn_t), full(wr_t), full(wrs_t),
                  full(wuk_t), tab, tab],
        out_specs=pl.BlockSpec((1, 1, QK_WIDTH, N_HEADS * tm), lambda i: (i // per, i % per, 0, 0)),
        out_shape=jax.ShapeDtypeStruct((batch, per, QK_WIDTH, N_HEADS * tm), BF16),
        compiler_params=_params("parallel"),
        name="q_prep_t",
    )(cq, gq, wn_t, wr_t, wrs_t, wuk_t, cos_qt, sin_qt)


def _softmax_step(s, v, m_ref, l_ref, acc_ref):
    m_prev = m_ref[...]
    m_new = jnp.maximum(m_prev, jnp.max(s, axis=-1, keepdims=True))
    alpha = jnp.exp(m_prev - m_new)
    p = jnp.exp(s - m_new)
    l_ref[...] = alpha * l_ref[...] + jnp.sum(p, axis=-1, keepdims=True)
    acc_ref[...] = alpha * acc_ref[...] + _dot(p.astype(BF16), v)
    m_ref[...] = m_new


def _value_up(o_lat, wuv_ref, t):
    cols = []
    for j in range(N_HEADS // 2):
        a = o_lat[(2 * j) * t:(2 * j + 1) * t].astype(BF16)
        b = o_lat[(2 * j + 1) * t:(2 * j + 2) * t].astype(BF16)
        cols.append(_dot(a, wuv_ref[2 * j]) + _dot(b, wuv_ref[2 * j + 1]))
    return jnp.concatenate(cols, axis=-1)


def _init_softmax(m_ref, l_ref, acc_ref):
    m_ref[...] = jnp.full(m_ref.shape, -jnp.inf, F32)
    l_ref[...] = jnp.zeros(l_ref.shape, F32)
    acc_ref[...] = jnp.zeros(acc_ref.shape, F32)


def _prompt_attn_kernel(qt_ref, k_ref, vt_ref, wuvt_ref, o_ref, m_ref, l_ref, acc_ref, *, tq, cg):
    qi = pl.program_id(1)
    cols = N_HEADS * tq
    _init_softmax(m_ref, l_ref, acc_ref)

    def chunk(kb, diagonal):
        k = k_ref[0, pl.ds(pl.multiple_of(kb * tq, tq), tq), :]
        vt = vt_ref[kb]
        if diagonal:
            key = lax.broadcasted_iota(jnp.int32, (tq, cg), 0)
            tok = lax.broadcasted_iota(jnp.int32, (tq, cg), 1) & (tq - 1)
            causal = key <= tok
        for g in range(cols // cg):
            sl = slice(g * cg, (g + 1) * cg)
            s = _dot(k, qt_ref[0, 0, :, sl]) * (ATTN_SCALE * LOG2_E)
            if diagonal:
                s = jnp.where(causal, s, NEG_INF)
            m_prev = m_ref[:, sl]
            m_new = jnp.maximum(m_prev, jnp.max(s, axis=0, keepdims=True))
            alpha = jnp.exp2(m_prev - m_new)
            p = jnp.exp2(s - m_new)
            l_ref[:, sl] = alpha * l_ref[:, sl] + jnp.sum(p, axis=0, keepdims=True)
            acc_ref[:, sl] = alpha * acc_ref[:, sl] + _dot(vt, p.astype(BF16))
            m_ref[:, sl] = m_new

    def body(kb, carry):
        chunk(kb, False)
        return carry

    lax.fori_loop(0, qi, body, 0)
    chunk(qi, True)
    heads = []
    for h in range(N_HEADS):
        sl = slice(h * tq, (h + 1) * tq)
        o_lat = (acc_ref[:, sl] / l_ref[:, sl]).astype(BF16)
        heads.append(_dot(wuvt_ref[h], o_lat))
    o_ref[0] = jnp.concatenate(heads, axis=0).T


def _prompt_attn(qt, kf, vt, wuv_t, tq, cg):
    b, nq, _, cols = qt.shape
    l = nq * tq
    return pl.pallas_call(
        functools.partial(_prompt_attn_kernel, tq=tq, cg=cg),
        grid=(b, nq),
        in_specs=[pl.BlockSpec((1, 1, QK_WIDTH, cols), lambda bi, qi: (bi, qi, 0, 0)),
                  pl.BlockSpec((1, l, QK_WIDTH), lambda bi, qi: (bi, 0, 0)),
                  pl.BlockSpec((nq, KV_LORA, tq), lambda bi, qi: (bi, 0, 0)),
                  pl.BlockSpec(wuv_t.shape, lambda bi, qi: (0, 0, 0))],
        out_specs=pl.BlockSpec((1, tq, ATTN_WIDTH), lambda bi, qi: (bi, qi, 0)),
        out_shape=jax.ShapeDtypeStruct((b, l, ATTN_WIDTH), F32),
        scratch_shapes=[pltpu.VMEM((1, cols), F32), pltpu.VMEM((1, cols), F32),
                        pltpu.VMEM((KV_LORA, cols), F32)],
        compiler_params=_params("parallel", "arbitrary"),
        name="prompt_attn",
    )(qt, kf, vt, wuv_t)


def _paged_attn_kernel(pt_ref, q_ref, ks_ref, rs_ref, wuv_ref, ckv_hbm, kr_hbm, o_ref,
                       m_ref, l_ref, acc_ref, kbuf, rbuf, sem, *, pps, page, t):
    b = pl.program_id(0)
    j = pl.program_id(1)
    nj = pl.num_programs(1)
    step = b * nj + j
    slot = step % 2
    rows = N_HEADS * t

    def page_copies(bb, jj, sl):
        for i in range(pps):
            pg = pt_ref[bb, jj * pps + i]
            yield pltpu.make_async_copy(ckv_hbm.at[pg], kbuf.at[sl, i], sem.at[0, sl])
            yield pltpu.make_async_copy(kr_hbm.at[pg], rbuf.at[sl, i], sem.at[1, sl])

    @pl.when(step == 0)
    def _():
        for cp in page_copies(b, j, slot):
            cp.start()

    @pl.when(step + 1 < pl.num_programs(0) * nj)
    def _():
        last = j == nj - 1
        for cp in page_copies(jnp.where(last, b + 1, b), jnp.where(last, 0, j + 1), 1 - slot):
            cp.start()

    q = q_ref[0].astype(BF16)
    q_lat = q[:, :KV_LORA]
    q_rope = q[:, KV_LORA:KV_LORA + QK_ROPE]

    @pl.when(j == 0)
    def _():
        _init_softmax(m_ref, l_ref, acc_ref)
        k = jnp.concatenate([ks_ref[0], jnp.zeros((page - t, KV_LORA), F32)], axis=0).astype(BF16)
        r = jnp.concatenate([rs_ref[0], jnp.zeros((page - t, QK_ROPE), F32)], axis=0).astype(BF16)
        s = (_dot_nt(q_lat, k) + _dot_nt(q_rope, r)) * ATTN_SCALE
        tok = lax.broadcasted_iota(jnp.int32, (t, page), 0)
        key = lax.broadcasted_iota(jnp.int32, (t, page), 1)
        s = jnp.where((key <= tok)[None], s.reshape(N_HEADS, t, page), NEG_INF).reshape(rows, page)
        _softmax_step(s, k, m_ref, l_ref, acc_ref)

    for cp in page_copies(b, j, slot):
        cp.wait()
    half = pps // 2
    for lo in (0, half):
        k = kbuf[slot, lo:lo + half].reshape(half * page, KV_LORA).astype(BF16)
        r_t = jnp.concatenate([rbuf[slot, i] for i in range(lo, lo + half)], axis=1).astype(BF16)
        s = (_dot_nt(q_lat, k) + _dot(q_rope, r_t)) * ATTN_SCALE
        _softmax_step(s, k, m_ref, l_ref, acc_ref)

    @pl.when(j == nj - 1)
    def _():
        o_ref[0] = _value_up(acc_ref[...] / l_ref[...], wuv_ref, t)


def _paged_attn(q, ckv_new, kr_new, cache_ckv, cache_kr, page_table, wuv, pps):
    bd, rows, _ = q.shape
    t = rows // N_HEADS
    _, page, _ = cache_ckv.shape
    n_pages = page_table.shape[1]
    assert n_pages % pps == 0 and pps % 2 == 0 and t <= page
    grid_spec = pltpu.PrefetchScalarGridSpec(
        num_scalar_prefetch=1,
        grid=(bd, n_pages // pps),
        in_specs=[pl.BlockSpec((1, rows, QK_WIDTH), lambda b, j, pt: (b, 0, 0)),
                  pl.BlockSpec((1, t, KV_LORA), lambda b, j, pt: (b, 0, 0)),
                  pl.BlockSpec((1, t, QK_ROPE), lambda b, j, pt: (b, 0, 0)),
                  pl.BlockSpec(wuv.shape, lambda b, j, pt: (0, 0, 0)),
                  pl.BlockSpec(memory_space=pl.ANY), pl.BlockSpec(memory_space=pl.ANY)],
        out_specs=pl.BlockSpec((1, t, ATTN_WIDTH), lambda b, j, pt: (b, 0, 0)),
        scratch_shapes=[pltpu.VMEM((rows, 1), F32), pltpu.VMEM((rows, 1), F32),
                        pltpu.VMEM((rows, KV_LORA), F32),
                        pltpu.VMEM((2, pps, page, KV_LORA), F32), pltpu.VMEM((2, pps, QK_ROPE, page), F32),
                        pltpu.SemaphoreType.DMA((2, 2))],
    )
    return pl.pallas_call(
        functools.partial(_paged_attn_kernel, pps=pps, page=page, t=t),
        grid_spec=grid_spec,
        out_shape=jax.ShapeDtypeStruct((bd, t, ATTN_WIDTH), F32),
        compiler_params=_params("arbitrary", "arbitrary"),
        name="paged_attn",
    )(page_table, q, ckv_new, kr_new, wuv, cache_ckv, cache_kr.transpose(0, 2, 1))


SET_GROUPS = LANE // SSM_GROUP
N_SETS = SSM_GROUPS // SET_GROUPS
SET_STATE = SET_GROUPS * SSM_STATE


def _cmul(xr, xi, ar, ai):
    return xr * ar - xi * ai, xr * ai + xi * ar


def _s5_kernel(*refs, tc, nc, has_h0):
    if has_h0:
        u_ref, d_ref, ws_ref, bz_ref, cz_ref, pw_ref, h0_ref, y_ref, hl_ref = refs
    else:
        u_ref, d_ref, ws_ref, bz_ref, cz_ref, pw_ref, y_ref, hl_ref = refs
    rows = u_ref.shape[0] // tc
    u_step = [u_ref[pl.ds(s, rows, stride=tc), :] for s in range(tc)]
    u_cat = jnp.concatenate([u.astype(BF16) for u in u_step], axis=1)
    z = _dot(u_cat, bz_ref[0])
    sr, si = z[:, :SET_STATE], z[:, SET_STATE:]
    if has_h0:
        assert nc == 1
        pr, pi = h0_ref[0, 0, :, :SET_STATE], h0_ref[0, 0, :, SET_STATE:]
        dr, di = _cmul(pr, pi, pw_ref[0, 0, 0:1], pw_ref[0, 0, 1:2])
        sr, si = sr + dr, si + di
    else:
        chunk = lax.broadcasted_iota(jnp.int32, (rows, SET_STATE), 0) & (nc - 1)
        for k in range(nc.bit_length() - 1):
            sh = 1 << k
            keep = chunk >= sh
            tr = jnp.where(keep, pltpu.roll(sr, sh, axis=0), 0.0)
            ti = jnp.where(keep, pltpu.roll(si, sh, axis=0), 0.0)
            dr, di = _cmul(tr, ti, pw_ref[0, k, 0:1], pw_ref[0, k, 1:2])
            sr, si = sr + dr, si + di
        pr = jnp.where(chunk >= 1, pltpu.roll(sr, 1, axis=0), 0.0)
        pi = jnp.where(chunk >= 1, pltpu.roll(si, 1, axis=0), 0.0)
    y_state = _dot(jnp.concatenate([pr, pi], axis=1).astype(BF16), cz_ref[0])
    for t in range(tc):
        y_t = y_state[:, t * LANE:(t + 1) * LANE] + _dot(u_cat[:, :(t + 1) * LANE], ws_ref[0, (tc - 1 - t) * LANE:, :])
        y_ref[pl.ds(t, rows, stride=tc), :] = y_t + d_ref[...] * u_step[t]
    s_last = jnp.concatenate([sr, si], axis=1)
    if nc == 1:
        hl_ref[0, 0] = s_last
    else:
        for q in range(rows // nc):
            hl_ref[0, 0, q:q + 1, :] = s_last[q * nc + nc - 1:q * nc + nc, :]


def _s5(u, d, ws, bz, cz, pw, h0, tc, nc, block_rows):
    n = u.shape[0]
    nblk = n // block_rows
    nseq = block_rows // (tc * nc)
    per_set = lambda a: pl.BlockSpec((1,) + a.shape[1:], lambda si, rb: (si,) + (0,) * (a.ndim - 1))
    tok = pl.BlockSpec((block_rows, LANE), lambda si, rb: (rb, si))
    state = pl.BlockSpec((1, 1, nseq, 2 * SET_STATE), lambda si, rb: (si, rb, 0, 0))
    skip = pl.BlockSpec((1, LANE), lambda si, rb: (0, si))
    ins = [u, d, ws, bz, cz, pw] + ([h0] if h0 is not None else [])
    return pl.pallas_call(
        functools.partial(_s5_kernel, tc=tc, nc=nc, has_h0=h0 is not None),
        grid=(N_SETS, nblk),
        in_specs=[tok, skip, per_set(ws), per_set(bz), per_set(cz), per_set(pw)]
                 + ([state] if h0 is not None else []),
        out_specs=[tok, state],
        out_shape=[jax.ShapeDtypeStruct((n, SSM_GROUPS * SSM_GROUP), F32),
                   jax.ShapeDtypeStruct((N_SETS, nblk, nseq, 2 * SET_STATE), F32)],
        compiler_params=_params("parallel", "parallel"),
        name="s5",
    )(*ins)


def _block_diag(x):
    a, b = x.shape[-2:]
    rows = x.reshape(x.shape[:-3] + (SET_GROUPS * a, b)).astype(BF16)
    spread = lax.broadcasted_iota(jnp.int32, (b, SET_GROUPS * b), 1) % b == lax.broadcasted_iota(
        jnp.int32, (b, SET_GROUPS * b), 0)
    wide = jnp.dot(rows, spread.astype(BF16), preferred_element_type=BF16)
    shape = (SET_GROUPS * a, SET_GROUPS * b)
    same = lax.broadcasted_iota(jnp.int32, shape, 0) // a == lax.broadcasted_iota(jnp.int32, shape, 1) // b
    return jnp.where(same, wide, jnp.zeros((), BF16))


def _s5_weights(lam_re, lam_im, log_dt, b_re, b_im, c_re, c_im, tc, n_levels):
    hi = lax.Precision.HIGHEST
    lam = lax.complex(lam_re.astype(F32), lam_im.astype(F32))
    ldt = lam * jnp.exp(log_dt.astype(F32))[:, None]
    lam_bar = jnp.exp(ldt)
    b_bar = ((lam_bar - 1.0) / lam)[..., None] * lax.complex(b_re.astype(F32), b_im.astype(F32))
    c = lax.complex(c_re.astype(F32), c_im.astype(F32))
    sets = lambda x: x.reshape(x.shape[:-3] + (N_SETS, SET_GROUPS) + x.shape[-2:])
    steps = jnp.arange(tc + 1, dtype=F32)
    pw = jnp.exp(ldt[None] * steps[:, None, None])
    kern = jnp.real(jnp.einsum('gop,dgp,gpi->dgio', c, pw[:tc], b_bar, precision=hi))
    ws = _block_diag(sets(kern[::-1])).transpose(1, 0, 2, 3).reshape(N_SETS, tc * LANE, LANE)
    bzc = (pw[tc - 1 - jnp.arange(tc)][..., None] * b_bar[None]).transpose(0, 1, 3, 2)
    bz = jnp.concatenate([_block_diag(sets(jnp.real(bzc))), _block_diag(sets(jnp.imag(bzc)))], axis=-1)
    bz = bz.transpose(1, 0, 2, 3).reshape(N_SETS, tc * LANE, 2 * SET_STATE)
    czc = (c[None] * pw[1:tc + 1][:, :, None, :]).transpose(0, 1, 3, 2)
    cz = jnp.concatenate([_block_diag(sets(jnp.real(czc))), _block_diag(sets(-jnp.imag(czc)))], axis=-2)
    cz = jnp.concatenate([cz[t] for t in range(tc)], axis=-1)
    lev = jnp.exp(ldt[None] * (tc * 2.0 ** jnp.arange(n_levels, dtype=F32))[:, None, None])
    pwl = jnp.stack([jnp.real(lev), jnp.imag(lev)], axis=1).reshape(n_levels, 2, N_SETS, SET_STATE)
    return ws, bz, cz, pwl.transpose(2, 0, 1, 3)


def _s5_branch(u, h0, ssm_w, d, batch, seq, tc):
    nc = seq // tc
    n_levels = max(nc.bit_length() - 1, 1)
    assert nc & (nc - 1) == 0 and (h0 is None or nc == 1)
    ws, bz, cz, pwl = _s5_weights(*ssm_w, tc, n_levels)
    block_rows = seq if nc > 1 else batch * seq
    nseq = block_rows // seq
    h0_t = None
    if h0 is not None:
        h0_t = h0.astype(F32).reshape(batch // nseq, nseq, N_SETS, SET_GROUPS, SSM_STATE, 2)
        h0_t = h0_t.transpose(2, 0, 1, 5, 3, 4).reshape(N_SETS, batch // nseq, nseq, 2 * SET_STATE)
    y, hl = _s5(u, d, ws, bz, cz, pwl, h0_t, tc, nc, block_rows)
    hl = hl.reshape(N_SETS, batch, 2, SET_GROUPS, SSM_STATE).transpose(1, 0, 3, 4, 2)
    return y, hl.reshape(batch, SSM_GROUPS, SSM_STATE, 2)


def _merge_kernel(x_ref, y_ref, gs_ref, o_ref, ga_ref, ms_ref, ma_ref,
                  wglu_ref, bglu_ref, wbs_ref, wba_ref, wout_ref, gf_ref, out_ref):
    gate = lambda ref: ref[...].astype(F32)
    zg = jax.nn.gelu(y_ref[...])
    glu = zg * jax.nn.sigmoid(_dot(zg.astype(BF16), wglu_ref[...]) + bglu_ref[...])
    y_s = _dot((glu * jax.nn.silu(gate(gs_ref))).astype(BF16), wbs_ref[...])
    y_a = _dot((o_ref[...] * jax.nn.silu(gate(ga_ref))).astype(BF16), wba_ref[...])
    merged = jax.nn.sigmoid(gate(ms_ref)) * y_s + jax.nn.sigmoid(gate(ma_ref)) * y_a
    h = x_ref[...] + _dot(merged.astype(BF16), wout_ref[...])
    out_ref[...] = _rms(h, gf_ref[...])


def _merge(x, y, gs, o, ga, ms, ma, wglu, bglu, wbs, wba, wout, gf, tm):
    n = x.shape[0]
    row = pl.BlockSpec((tm, D_MODEL), lambda i: (i, 0))
    full = lambda a: pl.BlockSpec(a.shape, lambda i: (0,) * a.ndim)
    consts = (wglu, bglu, wbs, wba, wout, gf)
    return pl.pallas_call(
        _merge_kernel,
        grid=(n // tm,),
        in_specs=[row] * 7 + [full(a) for a in consts],
        out_specs=row,
        out_shape=jax.ShapeDtypeStruct((n, D_MODEL), F32),
        compiler_params=_params("parallel"),
        name="merge",
    )(x, y, gs, o, ga, ms, ma, *consts)


def _rope_tables(pos, reps):
    half = QK_ROPE // 2
    inv = ROPE_BASE ** (-jnp.arange(half, dtype=F32) * (2.0 / QK_ROPE))
    ang = pos.astype(F32)[:, None] * inv[None, :]
    cos, sin = jnp.cos(ang), jnp.sin(ang)
    cos_t = _pad_last(jnp.concatenate([cos, cos], axis=-1), ROPE_PAD)
    sin_t = _pad_last(jnp.concatenate([-sin, sin], axis=-1), ROPE_PAD)
    return jnp.tile(cos_t, (reps, 1)), jnp.tile(sin_t, (reps, 1))


def _swap_halves(w):
    half = QK_ROPE // 2
    return jnp.concatenate([w[..., half:], w[..., :half]], axis=-1)


def _pad_last(w, width):
    return jnp.pad(w, [(0, 0)] * (w.ndim - 1) + [(0, width - w.shape[-1])])


def _rope_weight(w_rope):
    return _pad_last(w_rope, ROPE_PAD).reshape(Q_LORA, N_HEADS * ROPE_PAD).astype(BF16)


def _rope_weight_t(w_rope):
    return w_rope.transpose(1, 2, 0).reshape(N_HEADS * QK_ROPE, Q_LORA).astype(BF16)


def _pack_weights(w_in, mla_w_uq, mla_w_uk, mla_w_uv):
    b = [0]
    for w in (D_MODEL, D_MODEL, Q_LORA, KV_LORA, QK_ROPE, ATTN_WIDTH, D_MODEL, D_MODEL):
        b.append(b[-1] + w)
    w_kr = w_in[:, b[4]:b[5]]
    w_all = jnp.concatenate([w_in[:, :b[4]], _pad_last(w_kr, ROPE_PAD), _pad_last(_swap_halves(w_kr), ROPE_PAD),
                             w_in[:, b[5]:]], axis=-1).astype(BF16)
    w_nope = _pad_last(mla_w_uq[..., :QK_NOPE], LANE).reshape(Q_LORA, N_HEADS * LANE).astype(BF16)
    w_rope = mla_w_uq[..., QK_NOPE:]
    w_uk = jnp.pad(mla_w_uk.transpose(1, 2, 0), ((0, 0), (0, LANE - QK_NOPE), (0, 0))).astype(BF16)
    w_uv = mla_w_uv.transpose(1, 0, 2)
    lo = jnp.pad(w_uv, ((0, 0), (0, 0), (0, LANE - V_DIM)))
    hi = jnp.pad(w_uv, ((0, 0), (0, 0), (LANE - V_DIM, 0)))
    w_uv_pair = jnp.where((jnp.arange(N_HEADS) % 2 == 0)[:, None, None], lo, hi).astype(BF16)
    return dict(
        w_all=w_all, w_nope=w_nope, w_uk=w_uk, w_uv_pair=w_uv_pair,
        w_r=_rope_weight(w_rope), w_rs=_rope_weight(_swap_halves(w_rope)),
        w_r_t=_rope_weight_t(w_rope), w_rs_t=_rope_weight_t(_swap_halves(w_rope)),
        w_uv_t=w_uv.transpose(0, 2, 1).astype(BF16))


def _layer(x, pos, h0, attend, w, batch, seq, tc, tm):
    n = x.shape[0]
    cos_t, sin_t = _rope_tables(pos, max(tm // seq, 1))
    row = lambda v: v.reshape(1, -1).astype(F32)
    u, gs, cq, ckv, kr, kf, vt, ga, ms, ma = _in_proj(x, row(w["norm_in"]), w["w_all"], row(w["mla_kv_norm"]),
                                                      cos_t, sin_t, tm)
    o = attend(cq, cos_t, sin_t, ckv, kr, kf, vt).reshape(n, ATTN_WIDTH)
    y_ssm, state = _s5_branch(u, h0, w["ssm"], row(w["ssm_d"]), batch, seq, tc)
    y = _merge(x, y_ssm, gs, o, ga, ms, ma, w["ssm_w_glu"].astype(BF16), row(w["ssm_b_glu"]),
               w["w_br_ssm"].astype(BF16), w["w_br_attn"].astype(BF16), w["w_out"].astype(BF16),
               row(w["norm_final"]), tm)
    return y, ckv, kr, state


def kernel(x_prompt, x_sample, cache_ckv, cache_krope, state_ssm, page_table,
           norm_in, w_in, ssm_lambda_re, ssm_lambda_im, ssm_log_dt, ssm_b_re, ssm_b_im,
           ssm_c_re, ssm_c_im, ssm_d, ssm_w_glu, ssm_b_glu, w_br_ssm,
           mla_q_norm, mla_w_uq, mla_kv_norm, mla_w_uk, mla_w_uv, w_br_attn, w_out, norm_final):
    b, l, _ = x_prompt.shape
    bd, t, _ = x_sample.shape
    w = dict(norm_in=norm_in, mla_kv_norm=mla_kv_norm,
             ssm=(ssm_lambda_re, ssm_lambda_im, ssm_log_dt, ssm_b_re, ssm_b_im, ssm_c_re, ssm_c_im),
             ssm_d=ssm_d, ssm_w_glu=ssm_w_glu, ssm_b_glu=ssm_b_glu, w_br_ssm=w_br_ssm,
             w_br_attn=w_br_attn, w_out=w_out, norm_final=norm_final,
             **_pack_weights(w_in, mla_w_uq, mla_w_uk, mla_w_uv))
    tm = 256
    gq = mla_q_norm.reshape(1, -1).astype(F32)

    def attend_prompt(cq, cos_t, sin_t, ckv, kr, kf, vt):
        qt = _q_prep_t(cq, gq, w["w_nope"].T, w["w_r_t"], w["w_rs_t"], w["w_uk"].transpose(0, 2, 1),
                       cos_t.T[:QK_ROPE], sin_t.T[:QK_ROPE], tm, b, l)
        return _prompt_attn(qt, kf.reshape(b, l, QK_WIDTH), vt, w["w_uv_t"], tm, cg=4096)

    def attend_sample(cq, cos_t, sin_t, ckv, kr, kf, vt):
        q = _q_prep(cq, gq, w["w_nope"], w["w_r"], w["w_rs"], w["w_uk"], cos_t, sin_t, tm, bd, t, F32)
        return _paged_attn(q.reshape(bd, N_HEADS * t, QK_WIDTH), ckv.reshape(bd, t, KV_LORA),
                           kr.reshape(bd, t, QK_ROPE), cache_ckv, cache_krope, page_table, w["w_uv_pair"],
                           pps=32)

    y_p, ckv_p, kr_p, ssm_p = _layer(x_prompt.reshape(b * l, D_MODEL), jnp.arange(l), None, attend_prompt,
                                     w, b, l, 16, tm)
    y_s, ckv_s, kr_s, ssm_s = _layer(x_sample.reshape(bd * t, D_MODEL), PAST_LEN + jnp.arange(t), state_ssm,
                                     attend_sample, w, bd, t, t, tm)
    return (y_p.reshape(b, l, D_MODEL), y_s.reshape(bd, t, D_MODEL),
            ckv_p.reshape(b, l, KV_LORA), kr_p.reshape(b, l, QK_ROPE), ssm_p,
            ckv_s.reshape(bd, t, KV_LORA), kr_s.reshape(bd, t, QK_ROPE), ssm_s)
```

```python
import functools
import math

import jax
import jax.numpy as jnp
from jax import lax
from jax.experimental import pallas as pl
from jax.experimental.pallas import tpu as pltpu

D_MODEL = 1024
PAST_LEN = 16384
SSM_GROUP = 16
SSM_GROUPS = 64
SSM_STATE = 64
N_HEADS = 16
QK_NOPE = 64
QK_ROPE = 32
V_DIM = 64
Q_LORA = 384
KV_LORA = 256
ATTN_WIDTH = N_HEADS * V_DIM
ROPE_BASE = 10000.0
ATTN_SCALE = (QK_NOPE + QK_ROPE) ** -0.5
NEG_INF = -1e30
NORM_EPS = 1e-6
LOG2_E = math.log2(math.e)

LANE = 128
ROPE_PAD = LANE
QK_WIDTH = KV_LORA + ROPE_PAD
VMEM_LIMIT = 56 * 1024 * 1024

F32 = jnp.float32
BF16 = jnp.bfloat16

_C_U = 0
_C_GS = _C_U + D_MODEL
_C_CQ = _C_GS + D_MODEL
_C_CKV = _C_CQ + Q_LORA
_C_KR = _C_CKV + KV_LORA
_C_KRS = _C_KR + ROPE_PAD
_C_GA = _C_KRS + ROPE_PAD
_C_MS = _C_GA + ATTN_WIDTH
_C_MA = _C_MS + D_MODEL
_C_END = _C_MA + D_MODEL


def _rms(x, g):
    return x * lax.rsqrt(jnp.mean(x * x, axis=-1, keepdims=True) + NORM_EPS) * g


def _dot(a, b):
    return jnp.dot(a, b, preferred_element_type=F32)


def _dot_nt(a, b):
    return lax.dot_general(a, b, (((1,), (1,)), ((), ())), preferred_element_type=F32)


def _params(*sem):
    return pltpu.CompilerParams(dimension_semantics=sem, vmem_limit_bytes=VMEM_LIMIT)


def _in_proj_kernel(x_ref, gin_ref, w_ref, gkv_ref, cos_ref, sin_ref,
                    u_ref, gs_ref, cq_ref, ckv_ref, kr_ref, kf_ref, vt_ref, ga_ref, ms_ref, ma_ref):
    xn = _rms(x_ref[...], gin_ref[...]).astype(BF16)

    def proj(lo, hi):
        return _dot(xn, w_ref[:, lo:hi])

    u_ref[...] = proj(_C_U, _C_GS)
    gs_ref[...] = proj(_C_GS, _C_CQ).astype(BF16)
    cq_ref[...] = proj(_C_CQ, _C_CKV)
    ckv = _rms(proj(_C_CKV, _C_KR), gkv_ref[...])
    ckv_ref[...] = ckv
    kr = proj(_C_KR, _C_KRS) * cos_ref[...] + proj(_C_KRS, _C_GA) * sin_ref[...]
    kr_ref[...] = kr[:, :QK_ROPE]
    kf_ref[:, :KV_LORA] = ckv.astype(BF16)
    kf_ref[:, KV_LORA:] = kr.astype(BF16)
    vt_ref[0] = ckv.T.astype(BF16)
    ga_ref[...] = proj(_C_GA, _C_MS).astype(BF16)
    ms_ref[...] = proj(_C_MS, _C_MA).astype(BF16)
    ma_ref[...] = proj(_C_MA, _C_END).astype(BF16)


def _in_proj(x, gin, w_all, gkv, cos_k, sin_k, tm):
    n = x.shape[0]
    nblk = cos_k.shape[0] // tm
    row = lambda w: pl.BlockSpec((tm, w), lambda i: (i, 0))
    full = lambda a: pl.BlockSpec(a.shape, lambda i: (0,) * a.ndim)
    tab = pl.BlockSpec((tm, ROPE_PAD), lambda i: (i % nblk, 0))
    widths = (D_MODEL, D_MODEL, Q_LORA, KV_LORA, QK_ROPE, QK_WIDTH, None, ATTN_WIDTH, D_MODEL, D_MODEL)
    dtypes = (F32, BF16, F32, F32, F32, BF16, BF16, BF16, BF16, BF16)
    vt_spec = pl.BlockSpec((1, KV_LORA, tm), lambda i: (i, 0, 0))
    shape = lambda w: (n, w) if w else (n // tm, KV_LORA, tm)
    return pl.pallas_call(
        _in_proj_kernel,
        grid=(n // tm,),
        in_specs=[row(D_MODEL), full(gin), full(w_all), full(gkv), tab, tab],
        out_specs=[row(w) if w else vt_spec for w in widths],
        out_shape=[jax.ShapeDtypeStruct(shape(w), d) for w, d in zip(widths, dtypes)],
        compiler_params=_params("parallel"),
        name="in_proj",
    )(x, gin, w_all, gkv, cos_k, sin_k)


def _q_prep_kernel(cq_ref, gq_ref, wn_ref, wr_ref, wrs_ref, wuk_ref, cos_ref, sin_ref, q_ref):
    nb, _, t, _ = q_ref.shape
    cqn = _rms(cq_ref[...], gq_ref[...]).astype(BF16)
    qn = _dot(cqn, wn_ref[...]).astype(BF16)
    qr = _dot(cqn, wr_ref[...])
    qrs = _dot(cqn, wrs_ref[...])
    cos = cos_ref[...]
    sin = sin_ref[...]
    for h in range(N_HEADS):
        sl = slice(h * LANE, (h + 1) * LANE)
        q_lat = _dot(qn[:, sl], wuk_ref[h])
        q_rope = qr[:, sl] * cos + qrs[:, sl] * sin
        q_ref[:, h, :, :KV_LORA] = q_lat.reshape(nb, t, KV_LORA).astype(q_ref.dtype)
        q_ref[:, h, :, KV_LORA:] = q_rope.reshape(nb, t, ROPE_PAD).astype(q_ref.dtype)


def _q_prep(cq, gq, wn, wr, wrs, wuk, cos_q, sin_q, tm, batch, seq, out_dtype):
    n = cq.shape[0]
    nblk = cos_q.shape[0] // tm
    full = lambda a: pl.BlockSpec(a.shape, lambda i: (0,) * a.ndim)
    tab = pl.BlockSpec((tm, ROPE_PAD), lambda i: (i % nblk, 0))
    if seq >= tm:
        per = seq // tm
        out_spec = pl.BlockSpec((1, N_HEADS, tm, QK_WIDTH), lambda i: (i // per, 0, i % per, 0))
    else:
        out_spec = pl.BlockSpec((tm // seq, N_HEADS, seq, QK_WIDTH), lambda i: (i, 0, 0, 0))
    return pl.pallas_call(
        _q_prep_kernel,
        grid=(n // tm,),
        in_specs=[pl.BlockSpec((tm, Q_LORA), lambda i: (i, 0)), full(gq), full(wn), full(wr), full(wrs),
                  full(wuk), tab, tab],
        out_specs=out_spec,
        out_shape=jax.ShapeDtypeStruct((batch, N_HEADS, seq, QK_WIDTH), out_dtype),
        compiler_params=_params("parallel"),
        name="q_prep",
    )(cq, gq, wn, wr, wrs, wuk, cos_q, sin_q)


def _q_prep_t_kernel(cq_ref, gq_ref, wn_ref, wr_ref, wrs_ref, wuk_ref, cos_ref, sin_ref, q_ref):
    tm = cq_ref.shape[0]
    cqn = _rms(cq_ref[...], gq_ref[...]).astype(BF16)
    qn = _dot_nt(wn_ref[...], cqn).astype(BF16)
    qr = _dot_nt(wr_ref[...], cqn)
    qrs = _dot_nt(wrs_ref[...], cqn)
    cos = cos_ref[...]
    sin = sin_ref[...]
    q_ref[0, 0, KV_LORA + QK_ROPE:, :] = jnp.zeros((ROPE_PAD - QK_ROPE, N_HEADS * tm), BF16)
    for h in range(N_HEADS):
        sl = slice(h * LANE, (h + 1) * LANE)
        rl = slice(h * QK_ROPE, (h + 1) * QK_ROPE)
        cl = slice(h * tm, (h + 1) * tm)
        q_ref[0, 0, :KV_LORA, cl] = _dot(wuk_ref[h], qn[sl]).astype(BF16)
        q_ref[0, 0, KV_LORA:KV_LORA + QK_ROPE, cl] = (qr[rl] * cos + qrs[rl] * sin).astype(BF16)


def _q_prep_t(cq, gq, wn_t, wr_t, wrs_t, wuk_t, cos_qt, sin_qt, tm, batch, seq):
    n = cq.shape[0]
    per = seq // tm
    full = lambda a: pl.BlockSpec(a.shape, lambda i: (0,) * a.ndim)
    tab = pl.BlockSpec((QK_ROPE, tm), lambda i: (0, i % per))
    return pl.pallas_call(
        _q_prep_t_kernel,
        grid=(n // tm,),
        in_specs=[pl.BlockSpec((tm, Q_LORA), lambda i: (i, 0)), full(gq), full(wn_t), full(wr_t), full(wrs_t),
                  full(wuk_t), tab, tab],
        out_specs=pl.BlockSpec((1, 1, QK_WIDTH, N_HEADS * tm), lambda i: (i // per, i % per, 0, 0)),
        out_shape=jax.ShapeDtypeStruct((batch, per, QK_WIDTH, N_HEADS * tm), BF16),
        compiler_params=_params("parallel"),
        name="q_prep_t",
    )(cq, gq, wn_t, wr_t, wrs_t, wuk_t, cos_qt, sin_qt)


def _softmax_step(s, v, m_ref, l_ref, acc_ref):
    m_prev = m_ref[...]
    m_new = jnp.maximum(m_prev, jnp.max(s, axis=-1, keepdims=True))
    alpha = jnp.exp(m_prev - m_new)
    p = jnp.exp(s - m_new)
    l_ref[...] = alpha * l_ref[...] + jnp.sum(p, axis=-1, keepdims=True)
    acc_ref[...] = alpha * acc_ref[...] + _dot(p.astype(BF16), v)
    m_ref[...] = m_new


def _value_up(o_lat, wuv_ref, t):
    cols = []
    for j in range(N_HEADS // 2):
        a = o_lat[(2 * j) * t:(2 * j + 1) * t].astype(BF16)
        b = o_lat[(2 * j + 1) * t:(2 * j + 2) * t].astype(BF16)
        cols.append(_dot(a, wuv_ref[2 * j]) + _dot(b, wuv_ref[2 * j + 1]))
    return jnp.concatenate(cols, axis=-1)


def _init_softmax(m_ref, l_ref, acc_ref):
    m_ref[...] = jnp.full(m_ref.shape, -jnp.inf, F32)
    l_ref[...] = jnp.zeros(l_ref.shape, F32)
    acc_ref[...] = jnp.zeros(acc_ref.shape, F32)


def _prompt_attn_kernel(qt_ref, k_ref, vt_ref, wuvt_ref, o_ref, m_ref, l_ref, acc_ref, *, tq, cg):
    qi = pl.program_id(1)
    cols = N_HEADS * tq
    _init_softmax(m_ref, l_ref, acc_ref)

    def chunk(kb, diagonal):
        k = k_ref[0, pl.ds(pl.multiple_of(kb * tq, tq), tq), :]
        vt = vt_ref[kb]
        if diagonal:
            key = lax.broadcasted_iota(jnp.int32, (tq, cg), 0)
            tok = lax.broadcasted_iota(jnp.int32, (tq, cg), 1) & (tq - 1)
            causal = key <= tok
        for g in range(cols // cg):
            sl = slice(g * cg, (g + 1) * cg)
            s = _dot(k, qt_ref[0, 0, :, sl]) * (ATTN_SCALE * LOG2_E)
            if diagonal:
                s = jnp.where(causal, s, NEG_INF)
            m_prev = m_ref[:, sl]
            m_new = jnp.maximum(m_prev, jnp.max(s, axis=0, keepdims=True))
            alpha = jnp.exp2(m_prev - m_new)
            p = jnp.exp2(s - m_new)
            l_ref[:, sl] = alpha * l_ref[:, sl] + jnp.sum(p, axis=0, keepdims=True)
            acc_ref[:, sl] = alpha * acc_ref[:, sl] + _dot(vt, p.astype(BF16))
            m_ref[:, sl] = m_new

    def body(kb, carry):
        chunk(kb, False)
        return carry

    lax.fori_loop(0, qi, body, 0)
    chunk(qi, True)
    heads = []
    for h in range(N_HEADS):
        sl = slice(h * tq, (h + 1) * tq)
        o_lat = (acc_ref[:, sl] / l_ref[:, sl]).astype(BF16)
        heads.append(_dot(wuvt_ref[h], o_lat))
    o_ref[0] = jnp.concatenate(heads, axis=0).T


def _prompt_attn(qt, kf, vt, wuv_t, tq, cg):
    b, nq, _, cols = qt.shape
    l = nq * tq
    return pl.pallas_call(
        functools.partial(_prompt_attn_kernel, tq=tq, cg=cg),
        grid=(b, nq),
        in_specs=[pl.BlockSpec((1, 1, QK_WIDTH, cols), lambda bi, qi: (bi, qi, 0, 0)),
                  pl.BlockSpec((1, l, QK_WIDTH), lambda bi, qi: (bi, 0, 0)),
                  pl.BlockSpec((nq, KV_LORA, tq), lambda bi, qi: (bi, 0, 0)),
                  pl.BlockSpec(wuv_t.shape, lambda bi, qi: (0, 0, 0))],
        out_specs=pl.BlockSpec((1, tq, ATTN_WIDTH), lambda bi, qi: (bi, qi, 0)),
        out_shape=jax.ShapeDtypeStruct((b, l, ATTN_WIDTH), F32),
        scratch_shapes=[pltpu.VMEM((1, cols), F32), pltpu.VMEM((1, cols), F32),
                        pltpu.VMEM((KV_LORA, cols), F32)],
        compiler_params=_params("parallel", "arbitrary"),
        name="prompt_attn",
    )(qt, kf, vt, wuv_t)


def _paged_attn_kernel(pt_ref, q_ref, ks_ref, rs_ref, wuv_ref, ckv_hbm, kr_hbm, o_ref,
                       m_ref, l_ref, acc_ref, kbuf, rbuf, sem, *, pps, page, t):
    b = pl.program_id(0)
    j = pl.program_id(1)
    nj = pl.num_programs(1)
    step = b * nj + j
    slot = step % 2
    rows = N_HEADS * t

    def page_copies(bb, jj, sl):
        for i in range(pps):
            pg = pt_ref[bb, jj * pps + i]
            yield pltpu.make_async_copy(ckv_hbm.at[pg], kbuf.at[sl, i], sem.at[0, sl])
            yield pltpu.make_async_copy(kr_hbm.at[pg], rbuf.at[sl, i], sem.at[1, sl])

    @pl.when(step == 0)
    def _():
        for cp in page_copies(b, j, slot):
            cp.start()

    @pl.when(step + 1 < pl.num_programs(0) * nj)
    def _():
        last = j == nj - 1
        for cp in page_copies(jnp.where(last, b + 1, b), jnp.where(last, 0, j + 1), 1 - slot):
            cp.start()

    q = q_ref[0].astype(BF16)
    q_lat = q[:, :KV_LORA]
    q_rope = q[:, KV_LORA:KV_LORA + QK_ROPE]

    @pl.when(j == 0)
    def _():
        _init_softmax(m_ref, l_ref, acc_ref)
        k = jnp.concatenate([ks_ref[0], jnp.zeros((page - t, KV_LORA), F32)], axis=0).astype(BF16)
        r = jnp.concatenate([rs_ref[0], jnp.zeros((page - t, QK_ROPE), F32)], axis=0).astype(BF16)
        s = (_dot_nt(q_lat, k) + _dot_nt(q_rope, r)) * ATTN_SCALE
        tok = lax.broadcasted_iota(jnp.int32, (t, page), 0)
        key = lax.broadcasted_iota(jnp.int32, (t, page), 1)
        s = jnp.where((key <= tok)[None], s.reshape(N_HEADS, t, page), NEG_INF).reshape(rows, page)
        _softmax_step(s, k, m_ref, l_ref, acc_ref)

    for cp in page_copies(b, j, slot):
        cp.wait()
    half = pps // 2
    for lo in (0, half):
        k = kbuf[slot, lo:lo + half].reshape(half * page, KV_LORA).astype(BF16)
        r_t = jnp.concatenate([rbuf[slot, i] for i in range(lo, lo + half)], axis=1).astype(BF16)
        s = (_dot_nt(q_lat, k) + _dot(q_rope, r_t)) * ATTN_SCALE
        _softmax_step(s, k, m_ref, l_ref, acc_ref)

    @pl.when(j == nj - 1)
    def _():
        o_ref[0] = _value_up(acc_ref[...] / l_ref[...], wuv_ref, t)


def _paged_attn(q, ckv_new, kr_new, cache_ckv, cache_kr, page_table, wuv, pps):
    bd, rows, _ = q.shape
    t = rows // N_HEADS
    _, page, _ = cache_ckv.shape
    n_pages = page_table.shape[1]
    assert n_pages % pps == 0 and pps % 2 == 0 and t <= page
    grid_spec = pltpu.PrefetchScalarGridSpec(
        num_scalar_prefetch=1,
        grid=(bd, n_pages // pps),
        in_specs=[pl.BlockSpec((1, rows, QK_WIDTH), lambda b, j, pt: (b, 0, 0)),
                  pl.BlockSpec((1, t, KV_LORA), lambda b, j, pt: (b, 0, 0)),
                  pl.BlockSpec((1, t, QK_ROPE), lambda b, j, pt: (b, 0, 0)),
                  pl.BlockSpec(wuv.shape, lambda b, j, pt: (0, 0, 0)),
                  pl.BlockSpec(memory_space=pl.ANY), pl.BlockSpec(memory_space=pl.ANY)],
        out_specs=pl.BlockSpec((1, t, ATTN_WIDTH), lambda b, j, pt: (b, 0, 0)),
        scratch_shapes=[pltpu.VMEM((rows, 1), F32), pltpu.VMEM((rows, 1), F32),
                        pltpu.VMEM((rows, KV_LORA), F32),
                        pltpu.VMEM((2, pps, page, KV_LORA), F32), pltpu.VMEM((2, pps, QK_ROPE, page), F32),
                        pltpu.SemaphoreType.DMA((2, 2))],
    )
    return pl.pallas_call(
        functools.partial(_paged_attn_kernel, pps=pps, page=page, t=t),
        grid_spec=grid_spec,
        out_shape=jax.ShapeDtypeStruct((bd, t, ATTN_WIDTH), F32),
        compiler_params=_params("arbitrary", "arbitrary"),
        name="paged_attn",
    )(page_table, q, ckv_new, kr_new, wuv, cache_ckv, cache_kr.transpose(0, 2, 1))


SET_GROUPS = LANE // SSM_GROUP
N_SETS = SSM_GROUPS // SET_GROUPS
SET_STATE = SET_GROUPS * SSM_STATE


def _cmul(xr, xi, ar, ai):
    return xr * ar - xi * ai, xr * ai + xi * ar


def _s5_kernel(*refs, tc, nc, has_h0):
    if has_h0:
        u_ref, d_ref, ws_ref, bz_ref, cz_ref, pw_ref, h0_ref, y_ref, hl_ref = refs
    else:
        u_ref, d_ref, ws_ref, bz_ref, cz_ref, pw_ref, y_ref, hl_ref = refs
    rows = u_ref.shape[0] // tc
    u_step = [u_ref[pl.ds(s, rows, stride=tc), :] for s in range(tc)]
    u_cat = jnp.concatenate([u.astype(BF16) for u in u_step], axis=1)
    z = _dot(u_cat, bz_ref[0])
    sr, si = z[:, :SET_STATE], z[:, SET_STATE:]
    if has_h0:
        assert nc == 1
        pr, pi = h0_ref[0, 0, :, :SET_STATE], h0_ref[0, 0, :, SET_STATE:]
        dr, di = _cmul(pr, pi, pw_ref[0, 0, 0:1], pw_ref[0, 0, 1:2])
        sr, si = sr + dr, si + di
    else:
        chunk = lax.broadcasted_iota(jnp.int32, (rows, SET_STATE), 0) & (nc - 1)
        for k in range(nc.bit_length() - 1):
            sh = 1 << k
            keep = chunk >= sh
            tr = jnp.where(keep, pltpu.roll(sr, sh, axis=0), 0.0)
            ti = jnp.where(keep, pltpu.roll(si, sh, axis=0), 0.0)
            dr, di = _cmul(tr, ti, pw_ref[0, k, 0:1], pw_ref[0, k, 1:2])
            sr, si = sr + dr, si + di
        pr = jnp.where(chunk >= 1, pltpu.roll(sr, 1, axis=0), 0.0)
        pi = jnp.where(chunk >= 1, pltpu.roll(si, 1, axis=0), 0.0)
    y_state = _dot(jnp.concatenate([pr, pi], axis=1).astype(BF16), cz_ref[0])
    for t in range(tc):
        y_t = y_state[:, t * LANE:(t + 1) * LANE] + _dot(u_cat[:, :(t + 1) * LANE], ws_ref[0, (tc - 1 - t) * LANE:, :])
        y_ref[pl.ds(t, rows, stride=tc), :] = y_t + d_ref[...] * u_step[t]
    s_last = jnp.concatenate([sr, si], axis=1)
    if nc == 1:
        hl_ref[0, 0] = s_last
    else:
        for q in range(rows // nc):
            hl_ref[0, 0, q:q + 1, :] = s_last[q * nc + nc - 1:q * nc + nc, :]


def _s5(u, d, ws, bz, cz, pw, h0, tc, nc, block_rows):
    n = u.shape[0]
    nblk = n // block_rows
    nseq = block_rows // (tc * nc)
    per_set = lambda a: pl.BlockSpec((1,) + a.shape[1:], lambda si, rb: (si,) + (0,) * (a.ndim - 1))
    tok = pl.BlockSpec((block_rows, LANE), lambda si, rb: (rb, si))
    state = pl.BlockSpec((1, 1, nseq, 2 * SET_STATE), lambda si, rb: (si, rb, 0, 0))
    skip = pl.BlockSpec((1, LANE), lambda si, rb: (0, si))
    ins = [u, d, ws, bz, cz, pw] + ([h0] if h0 is not None else [])
    return pl.pallas_call(
        functools.partial(_s5_kernel, tc=tc, nc=nc, has_h0=h0 is not None),
        grid=(N_SETS, nblk),
        in_specs=[tok, skip, per_set(ws), per_set(bz), per_set(cz), per_set(pw)]
                 + ([state] if h0 is not None else []),
        out_specs=[tok, state],
        out_shape=[jax.ShapeDtypeStruct((n, SSM_GROUPS * SSM_GROUP), F32),
                   jax.ShapeDtypeStruct((N_SETS, nblk, nseq, 2 * SET_STATE), F32)],
        compiler_params=_params("parallel", "parallel"),
        name="s5",
    )(*ins)


def _block_diag(x):
    a, b = x.shape[-2:]
    rows = x.reshape(x.shape[:-3] + (SET_GROUPS * a, b)).astype(BF16)
    spread = lax.broadcasted_iota(jnp.int32, (b, SET_GROUPS * b), 1) % b == lax.broadcasted_iota(
        jnp.int32, (b, SET_GROUPS * b), 0)
    wide = jnp.dot(rows, spread.astype(BF16), preferred_element_type=BF16)
    shape = (SET_GROUPS * a, SET_GROUPS * b)
    same = lax.broadcasted_iota(jnp.int32, shape, 0) // a == lax.broadcasted_iota(jnp.int32, shape, 1) // b
    return jnp.where(same, wide, jnp.zeros((), BF16))


def _s5_weights(lam_re, lam_im, log_dt, b_re, b_im, c_re, c_im, tc, n_levels):
    hi = lax.Precision.HIGHEST
    lam = lax.complex(lam_re.astype(F32), lam_im.astype(F32))
    ldt = lam * jnp.exp(log_dt.astype(F32))[:, None]
    lam_bar = jnp.exp(ldt)
    b_bar = ((lam_bar - 1.0) / lam)[..., None] * lax.complex(b_re.astype(F32), b_im.astype(F32))
    c = lax.complex(c_re.astype(F32), c_im.astype(F32))
    sets = lambda x: x.reshape(x.shape[:-3] + (N_SETS, SET_GROUPS) + x.shape[-2:])
    steps = jnp.arange(tc + 1, dtype=F32)
    pw = jnp.exp(ldt[None] * steps[:, None, None])
    kern = jnp.real(jnp.einsum('gop,dgp,gpi->dgio', c, pw[:tc], b_bar, precision=hi))
    ws = _block_diag(sets(kern[::-1])).transpose(1, 0, 2, 3).reshape(N_SETS, tc * LANE, LANE)
    bzc = (pw[tc - 1 - jnp.arange(tc)][..., None] * b_bar[None]).transpose(0, 1, 3, 2)
    bz = jnp.concatenate([_block_diag(sets(jnp.real(bzc))), _block_diag(sets(jnp.imag(bzc)))], axis=-1)
    bz = bz.transpose(1, 0, 2, 3).reshape(N_SETS, tc * LANE, 2 * SET_STATE)
    czc = (c[None] * pw[1:tc + 1][:, :, None, :]).transpose(0, 1, 3, 2)
    cz = jnp.concatenate([_block_diag(sets(jnp.real(czc))), _block_diag(sets(-jnp.imag(czc)))], axis=-2)
    cz = jnp.concatenate([cz[t] for t in range(tc)], axis=-1)
    lev = jnp.exp(ldt[None] * (tc * 2.0 ** jnp.arange(n_levels, dtype=F32))[:, None, None])
    pwl = jnp.stack([jnp.real(lev), jnp.imag(lev)], axis=1).reshape(n_levels, 2, N_SETS, SET_STATE)
    return ws, bz, cz, pwl.transpose(2, 0, 1, 3)


def _s5_branch(u, h0, ssm_w, d, batch, seq, tc):
    nc = seq // tc
    n_levels = max(nc.bit_length() - 1, 1)
    assert nc & (nc - 1) == 0 and (h0 is None or nc == 1)
    ws, bz, cz, pwl = _s5_weights(*ssm_w, tc, n_levels)
    block_rows = seq if nc > 1 else batch * seq
    nseq = block_rows // seq
    h0_t = None
    if h0 is not None:
        h0_t = h0.astype(F32).reshape(batch // nseq, nseq, N_SETS, SET_GROUPS, SSM_STATE, 2)
        h0_t = h0_t.transpose(2, 0, 1, 5, 3, 4).reshape(N_SETS, batch // nseq, nseq, 2 * SET_STATE)
    y, hl = _s5(u, d, ws, bz, cz, pwl, h0_t, tc, nc, block_rows)
    hl = hl.reshape(N_SETS, batch, 2, SET_GROUPS, SSM_STATE).transpose(1, 0, 3, 4, 2)
    return y, hl.reshape(batch, SSM_GROUPS, SSM_STATE, 2)


def _merge_kernel(x_ref, y_ref, gs_ref, o_ref, ga_ref, ms_ref, ma_ref,
                  wglu_ref, bglu_ref, wbs_ref, wba_ref, wout_ref, gf_ref, out_ref):
    gate = lambda ref: ref[...].astype(F32)
    zg = jax.nn.gelu(y_ref[...])
    glu = zg * jax.nn.sigmoid(_dot(zg.astype(BF16), wglu_ref[...]) + bglu_ref[...])
    y_s = _dot((glu * jax.nn.silu(gate(gs_ref))).astype(BF16), wbs_ref[...])
    y_a = _dot((o_ref[...] * jax.nn.silu(gate(ga_ref))).astype(BF16), wba_ref[...])
    merged = jax.nn.sigmoid(gate(ms_ref)) * y_s + jax.nn.sigmoid(gate(ma_ref)) * y_a
    h = x_ref[...] + _dot(merged.astype(BF16), wout_ref[...])
    out_ref[...] = _rms(h, gf_ref[...])


def _merge(x, y, gs, o, ga, ms, ma, wglu, bglu, wbs, wba, wout, gf, tm):
    n = x.shape[0]
    row = pl.BlockSpec((tm, D_MODEL), lambda i: (i, 0))
    full = lambda a: pl.BlockSpec(a.shape, lambda i: (0,) * a.ndim)
    consts = (wglu, bglu, wbs, wba, wout, gf)
    return pl.pallas_call(
        _merge_kernel,
        grid=(n // tm,),
        in_specs=[row] * 7 + [full(a) for a in consts],
        out_specs=row,
        out_shape=jax.ShapeDtypeStruct((n, D_MODEL), F32),
        compiler_params=_params("parallel"),
        name="merge",
    )(x, y, gs, o, ga, ms, ma, *consts)


def _rope_tables(pos, reps):
    half = QK_ROPE // 2
    inv = ROPE_BASE ** (-jnp.arange(half, dtype=F32) * (2.0 / QK_ROPE))
    ang = pos.astype(F32)[:, None] * inv[None, :]
    cos, sin = jnp.cos(ang), jnp.sin(ang)
    cos_t = _pad_last(jnp.concatenate([cos, cos], axis=-1), ROPE_PAD)
    sin_t = _pad_last(jnp.concatenate([-sin, sin], axis=-1), ROPE_PAD)
    return jnp.tile(cos_t, (reps, 1)), jnp.tile(sin_t, (reps, 1))


def _swap_halves(w):
    half = QK_ROPE // 2
    return jnp.concatenate([w[..., half:], w[..., :half]], axis=-1)


def _pad_last(w, width):
    return jnp.pad(w, [(0, 0)] * (w.ndim - 1) + [(0, width - w.shape[-1])])


def _rope_weight(w_rope):
    return _pad_last(w_rope, ROPE_PAD).reshape(Q_LORA, N_HEADS * ROPE_PAD).astype(BF16)


def _rope_weight_t(w_rope):
    return w_rope.transpose(1, 2, 0).reshape(N_HEADS * QK_ROPE, Q_LORA).astype(BF16)


def _pack_weights(w_in, mla_w_uq, mla_w_uk, mla_w_uv):
    b = [0]
    for w in (D_MODEL, D_MODEL, Q_LORA, KV_LORA, QK_ROPE, ATTN_WIDTH, D_MODEL, D_MODEL):
        b.append(b[-1] + w)
    w_kr = w_in[:, b[4]:b[5]]
    w_all = jnp.concatenate([w_in[:, :b[4]], _pad_last(w_kr, ROPE_PAD), _pad_last(_swap_halves(w_kr), ROPE_PAD),
                             w_in[:, b[5]:]], axis=-1).astype(BF16)
    w_nope = _pad_last(mla_w_uq[..., :QK_NOPE], LANE).reshape(Q_LORA, N_HEADS * LANE).astype(BF16)
    w_rope = mla_w_uq[..., QK_NOPE:]
    w_uk = jnp.pad(mla_w_uk.transpose(1, 2, 0), ((0, 0), (0, LANE - QK_NOPE), (0, 0))).astype(BF16)
    w_uv = mla_w_uv.transpose(1, 0, 2)
    lo = jnp.pad(w_uv, ((0, 0), (0, 0), (0, LANE - V_DIM)))
    hi = jnp.pad(w_uv, ((0, 0), (0, 0), (LANE - V_DIM, 0)))
    w_uv_pair = jnp.where((jnp.arange(N_HEADS) % 2 == 0)[:, None, None], lo, hi).astype(BF16)
    return dict(
        w_all=w_all, w_nope=w_nope, w_uk=w_uk, w_uv_pair=w_uv_pair,
        w_r=_rope_weight(w_rope), w_rs=_rope_weight(_swap_halves(w_rope)),
        w_r_t=_rope_weight_t(w_rope), w_rs_t=_rope_weight_t(_swap_halves(w_rope)),
        w_uv_t=w_uv.transpose(0, 2, 1).astype(BF16))


def _layer(x, pos, h0, attend, w, batch, seq, tc, tm):
    n = x.shape[0]
    cos_t, sin_t = _rope_tables(pos, max(tm // seq, 1))
    row = lambda v: v.reshape(1, -1).astype(F32)
    u, gs, cq, ckv, kr, kf, vt, ga, ms, ma = _in_proj(x, row(w["norm_in"]), w["w_all"], row(w["mla_kv_norm"]),
                                                      cos_t, sin_t, tm)
    o = attend(cq, cos_t, sin_t, ckv, kr, kf, vt).reshape(n, ATTN_WIDTH)
    y_ssm, state = _s5_branch(u, h0, w["ssm"], row(w["ssm_d"]), batch, seq, tc)
    y = _merge(x, y_ssm, gs, o, ga, ms, ma, w["ssm_w_glu"].astype(BF16), row(w["ssm_b_glu"]),
               w["w_br_ssm"].astype(BF16), w["w_br_attn"].astype(BF16), w["w_out"].astype(BF16),
               row(w["norm_final"]), tm)
    return y, ckv, kr, state


def kernel(x_prompt, x_sample, cache_ckv, cache_krope, state_ssm, page_table,
           norm_in, w_in, ssm_lambda_re, ssm_lambda_im, ssm_log_dt, ssm_b_re, ssm_b_im,
           ssm_c_re, ssm_c_im, ssm_d, ssm_w_glu, ssm_b_glu, w_br_ssm,
           mla_q_norm, mla_w_uq, mla_kv_norm, mla_w_uk, mla_w_uv, w_br_attn, w_out, norm_final):
    b, l, _ = x_prompt.shape
    bd, t, _ = x_sample.shape
    w = dict(norm_in=norm_in, mla_kv_norm=mla_kv_norm,
             ssm=(ssm_lambda_re, ssm_lambda_im, ssm_log_dt, ssm_b_re, ssm_b_im, ssm_c_re, ssm_c_im),
             ssm_d=ssm_d, ssm_w_glu=ssm_w_glu, ssm_b_glu=ssm_b_glu, w_br_ssm=w_br_ssm,
             w_br_attn=w_br_attn, w_out=w_out, norm_final=norm_final,
             **_pack_weights(w_in, mla_w_uq, mla_w_uk, mla_w_uv))
    tm = 256
    gq = mla_q_norm.reshape(1, -1).astype(F32)

    def attend_prompt(cq, cos_t, sin_t, ckv, kr, kf, vt):
        qt = _q_prep_t(cq, gq, w["w_nope"].T, w["w_r_t"], w["w_rs_t"], w["w_uk"].transpose(0, 2, 1),
                       cos_t.T[:QK_ROPE], sin_t.T[:QK_ROPE], tm, b, l)
        return _prompt_attn(qt, kf.reshape(b, l, QK_WIDTH), vt, w["w_uv_t"], tm, cg=4096)

    def attend_sample(cq, cos_t, sin_t, ckv, kr, kf, vt):
        q = _q_prep(cq, gq, w["w_nope"], w["w_r"], w["w_rs"], w["w_uk"], cos_t, sin_t, tm, bd, t, F32)
        return _paged_attn(q.reshape(bd, N_HEADS * t, QK_WIDTH), ckv.reshape(bd, t, KV_LORA),
                           kr.reshape(bd, t, QK_ROPE), cache_ckv, cache_krope, page_table, w["w_uv_pair"],
                           pps=64)

    y_p, ckv_p, kr_p, ssm_p = _layer(x_prompt.reshape(b * l, D_MODEL), jnp.arange(l), None, attend_prompt,
                                     w, b, l, 16, tm)
    y_s, ckv_s, kr_s, ssm_s = _layer(x_sample.reshape(bd * t, D_MODEL), PAST_LEN + jnp.arange(t), state_ssm,
                                     attend_sample, w, bd, t, t, tm)
    return (y_p.reshape(b, l, D_MODEL), y_s.reshape(bd, t, D_MODEL),
            ckv_p.reshape(b, l, KV_LORA), kr_p.reshape(b, l, QK_ROPE), ssm_p,
            ckv_s.reshape(bd, t, KV_LORA), kr_s.reshape(bd, t, QK_ROPE), ssm_s)
```

```python
import functools
import math

import jax
import jax.numpy as jnp
from jax import lax
from jax.experimental import pallas as pl
from jax.experimental.pallas import tpu as pltpu

D_MODEL = 1024
PAST_LEN = 16384
SSM_GROUP = 16
SSM_GROUPS = 64
SSM_STATE = 64
N_HEADS = 16
QK_NOPE = 64
QK_ROPE = 32
V_DIM = 64
Q_LORA = 384
KV_LORA = 256
ATTN_WIDTH = N_HEADS * V_DIM
ROPE_BASE = 10000.0
ATTN_SCALE = (QK_NOPE + QK_ROPE) ** -0.5
NEG_INF = -1e30
NORM_EPS = 1e-6
LOG2_E = math.log2(math.e)

LANE = 128
ROPE_PAD = LANE
QK_WIDTH = KV_LORA + ROPE_PAD
VMEM_LIMIT = 56 * 1024 * 1024

F32 = jnp.float32
BF16 = jnp.bfloat16

_C_U = 0
_C_GS = _C_U + D_MODEL
_C_CQ = _C_GS + D_MODEL
_C_CKV = _C_CQ + Q_LORA
_C_KR = _C_CKV + KV_LORA
_C_KRS = _C_KR + ROPE_PAD
_C_GA = _C_KRS + ROPE_PAD
_C_MS = _C_GA + ATTN_WIDTH
_C_MA = _C_MS + D_MODEL
_C_END = _C_MA + D_MODEL


def _rms(x, g):
    return x * lax.rsqrt(jnp.mean(x * x, axis=-1, keepdims=True) + NORM_EPS) * g


def _dot(a, b):
    return jnp.dot(a, b, preferred_element_type=F32)


def _dot_nt(a, b):
    return lax.dot_general(a, b, (((1,), (1,)), ((), ())), preferred_element_type=F32)


def _params(*sem):
    return pltpu.CompilerParams(dimension_semantics=sem, vmem_limit_bytes=VMEM_LIMIT)


def _in_proj_kernel(x_ref, gin_ref, w_ref, gkv_ref, cos_ref, sin_ref,
                    u_ref, gs_ref, cq_ref, ckv_ref, kr_ref, kf_ref, vt_ref, ga_ref, ms_ref, ma_ref):
    xn = _rms(x_ref[...], gin_ref[...]).astype(BF16)

    def proj(lo, hi):
        return _dot(xn, w_ref[:, lo:hi])

    u_ref[...] = proj(_C_U, _C_GS)
    gs_ref[...] = proj(_C_GS, _C_CQ).astype(BF16)
    cq_ref[...] = proj(_C_CQ, _C_CKV)
    ckv = _rms(proj(_C_CKV, _C_KR), gkv_ref[...])
    ckv_ref[...] = ckv
    kr = proj(_C_KR, _C_KRS) * cos_ref[...] + proj(_C_KRS, _C_GA) * sin_ref[...]
    kr_ref[...] = kr[:, :QK_ROPE]
    kf_ref[:, :KV_LORA] = ckv.astype(BF16)
    kf_ref[:, KV_LORA:] = kr.astype(BF16)
    vt_ref[0] = ckv.T.astype(BF16)
    ga_ref[...] = proj(_C_GA, _C_MS).astype(BF16)
    ms_ref[...] = proj(_C_MS, _C_MA).astype(BF16)
    ma_ref[...] = proj(_C_MA, _C_END).astype(BF16)


def _in_proj(x, gin, w_all, gkv, cos_k, sin_k, tm):
    n = x.shape[0]
    nblk = cos_k.shape[0] // tm
    row = lambda w: pl.BlockSpec((tm, w), lambda i: (i, 0))
    full = lambda a: pl.BlockSpec(a.shape, lambda i: (0,) * a.ndim)
    tab = pl.BlockSpec((tm, ROPE_PAD), lambda i: (i % nblk, 0))
    widths = (D_MODEL, D_MODEL, Q_LORA, KV_LORA, QK_ROPE, QK_WIDTH, None, ATTN_WIDTH, D_MODEL, D_MODEL)
    dtypes = (F32, BF16, F32, F32, F32, BF16, BF16, BF16, BF16, BF16)
    vt_spec = pl.BlockSpec((1, KV_LORA, tm), lambda i: (i, 0, 0))
    shape = lambda w: (n, w) if w else (n // tm, KV_LORA, tm)
    return pl.pallas_call(
        _in_proj_kernel,
        grid=(n // tm,),
        in_specs=[row(D_MODEL), full(gin), full(w_all), full(gkv), tab, tab],
        out_specs=[row(w) if w else vt_spec for w in widths],
        out_shape=[jax.ShapeDtypeStruct(shape(w), d) for w, d in zip(widths, dtypes)],
        compiler_params=_params("parallel"),
        name="in_proj",
    )(x, gin, w_all, gkv, cos_k, sin_k)


def _q_prep_kernel(cq_ref, gq_ref, wn_ref, wr_ref, wrs_ref, wuk_ref, cos_ref, sin_ref, q_ref):
    nb, _, t, _ = q_ref.shape
    cqn = _rms(cq_ref[...], gq_ref[...]).astype(BF16)
    qn = _dot(cqn, wn_ref[...]).astype(BF16)
    qr = _dot(cqn, wr_ref[...])
    qrs = _dot(cqn, wrs_ref[...])
    cos = cos_ref[...]
    sin = sin_ref[...]
    for h in range(N_HEADS):
        sl = slice(h * LANE, (h + 1) * LANE)
        q_lat = _dot(qn[:, sl], wuk_ref[h])
        q_rope = qr[:, sl] * cos + qrs[:, sl] * sin
        q_ref[:, h, :, :KV_LORA] = q_lat.reshape(nb, t, KV_LORA).astype(q_ref.dtype)
        q_ref[:, h, :, KV_LORA:] = q_rope.reshape(nb, t, ROPE_PAD).astype(q_ref.dtype)


def _q_prep(cq, gq, wn, wr, wrs, wuk, cos_q, sin_q, tm, batch, seq, out_dtype):
    n = cq.shape[0]
    nblk = cos_q.shape[0] // tm
    full = lambda a: pl.BlockSpec(a.shape, lambda i: (0,) * a.ndim)
    tab = pl.BlockSpec((tm, ROPE_PAD), lambda i: (i % nblk, 0))
    if seq >= tm:
        per = seq // tm
        out_spec = pl.BlockSpec((1, N_HEADS, tm, QK_WIDTH), lambda i: (i // per, 0, i % per, 0))
    else:
        out_spec = pl.BlockSpec((tm // seq, N_HEADS, seq, QK_WIDTH), lambda i: (i, 0, 0, 0))
    return pl.pallas_call(
        _q_prep_kernel,
        grid=(n // tm,),
        in_specs=[pl.BlockSpec((tm, Q_LORA), lambda i: (i, 0)), full(gq), full(wn), full(wr), full(wrs),
                  full(wuk), tab, tab],
        out_specs=out_spec,
        out_shape=jax.ShapeDtypeStruct((batch, N_HEADS, seq, QK_WIDTH), out_dtype),
        compiler_params=_params("parallel"),
        name="q_prep",
    )(cq, gq, wn, wr, wrs, wuk, cos_q, sin_q)


def _q_prep_t_kernel(cq_ref, gq_ref, wn_ref, wr_ref, wrs_ref, wuk_ref, cos_ref, sin_ref, q_ref):
    tm = cq_ref.shape[0]
    cqn = _rms(cq_ref[...], gq_ref[...]).astype(BF16)
    qn = _dot_nt(wn_ref[...], cqn).astype(BF16)
    qr = _dot_nt(wr_ref[...], cqn)
    qrs = _dot_nt(wrs_ref[...], cqn)
    cos = cos_ref[...]
    sin = sin_ref[...]
    q_ref[0, 0, KV_LORA + QK_ROPE:, :] = jnp.zeros((ROPE_PAD - QK_ROPE, N_HEADS * tm), BF16)
    for h in range(N_HEADS):
        sl = slice(h * LANE, (h + 1) * LANE)
        rl = slice(h * QK_ROPE, (h + 1) * QK_ROPE)
        cl = slice(h * tm, (h + 1) * tm)
        q_ref[0, 0, :KV_LORA, cl] = _dot(wuk_ref[h], qn[sl]).astype(BF16)
        q_ref[0, 0, KV_LORA:KV_LORA + QK_ROPE, cl] = (qr[rl] * cos + qrs[rl] * sin).astype(BF16)


def _q_prep_t(cq, gq, wn_t, wr_t, wrs_t, wuk_t, cos_qt, sin_qt, tm, batch, seq):
    n = cq.shape[0]
    per = seq // tm
    full = lambda a: pl.BlockSpec(a.shape, lambda i: (0,) * a.ndim)
    tab = pl.BlockSpec((QK_ROPE, tm), lambda i: (0, i % per))
    return pl.pallas_call(
        _q_prep_t_kernel,
        grid=(n // tm,),
        in_specs=[pl.BlockSpec((tm, Q_LORA), lambda i: (i, 0)), full(gq), full(wn_t), full(wr_t), full(wrs_t),
                  full(wuk_t), tab, tab],
        out_specs=pl.BlockSpec((1, 1, QK_WIDTH, N_HEADS * tm), lambda i: (i // per, i % per, 0, 0)),
        out_shape=jax.ShapeDtypeStruct((batch, per, QK_WIDTH, N_HEADS * tm), BF16),
        compiler_params=_params("parallel"),
        name="q_prep_t",
    )(cq, gq, wn_t, wr_t, wrs_t, wuk_t, cos_qt, sin_qt)


def _softmax_step(s, v, m_ref, l_ref, acc_ref):
    m_prev = m_ref[...]
    m_new = jnp.maximum(m_prev, jnp.max(s, axis=-1, keepdims=True))
    alpha = jnp.exp(m_prev - m_new)
    p = jnp.exp(s - m_new)
    l_ref[...] = alpha * l_ref[...] + jnp.sum(p, axis=-1, keepdims=True)
    acc_ref[...] = alpha * acc_ref[...] + _dot(p.astype(BF16), v)
    m_ref[...] = m_new


def _value_up(o_lat, wuv_ref, t):
    cols = []
    for j in range(N_HEADS // 2):
        a = o_lat[(2 * j) * t:(2 * j + 1) * t].astype(BF16)
        b = o_lat[(2 * j + 1) * t:(2 * j + 2) * t].astype(BF16)
        cols.append(_dot(a, wuv_ref[2 * j]) + _dot(b, wuv_ref[2 * j + 1]))
    return jnp.concatenate(cols, axis=-1)


def _init_softmax(m_ref, l_ref, acc_ref):
    m_ref[...] = jnp.full(m_ref.shape, -jnp.inf, F32)
    l_ref[...] = jnp.zeros(l_ref.shape, F32)
    acc_ref[...] = jnp.zeros(acc_ref.shape, F32)


def _prompt_attn_kernel(qt_ref, k_ref, vt_ref, wuvt_ref, o_ref, m_ref, l_ref, acc_ref, *, tq, cg):
    qi = pl.program_id(1)
    cols = N_HEADS * tq
    _init_softmax(m_ref, l_ref, acc_ref)

    def chunk(kb, diagonal):
        k = k_ref[0, pl.ds(pl.multiple_of(kb * tq, tq), tq), :]
        vt = vt_ref[kb]
        if diagonal:
            key = lax.broadcasted_iota(jnp.int32, (tq, cg), 0)
            tok = lax.broadcasted_iota(jnp.int32, (tq, cg), 1) & (tq - 1)
            causal = key <= tok
        for g in range(cols // cg):
            sl = slice(g * cg, (g + 1) * cg)
            s = _dot(k, qt_ref[0, 0, :, sl]) * (ATTN_SCALE * LOG2_E)
            if diagonal:
                s = jnp.where(causal, s, NEG_INF)
            m_prev = m_ref[:, sl]
            m_new = jnp.maximum(m_prev, jnp.max(s, axis=0, keepdims=True))
            alpha = jnp.exp2(m_prev - m_new)
            p = jnp.exp2(s - m_new)
            l_ref[:, sl] = alpha * l_ref[:, sl] + jnp.sum(p, axis=0, keepdims=True)
            acc_ref[:, sl] = alpha * acc_ref[:, sl] + _dot(vt, p.astype(BF16))
            m_ref[:, sl] = m_new

    def body(kb, carry):
        chunk(kb, False)
        return carry

    lax.fori_loop(0, qi, body, 0)
    chunk(qi, True)
    heads = []
    for h in range(N_HEADS):
        sl = slice(h * tq, (h + 1) * tq)
        o_lat = (acc_ref[:, sl] / l_ref[:, sl]).astype(BF16)
        heads.append(_dot(wuvt_ref[h], o_lat))
    o_ref[0] = jnp.concatenate(heads, axis=0).T


def _prompt_attn(qt, kf, vt, wuv_t, tq, cg):
    b, nq, _, cols = qt.shape
    l = nq * tq
    return pl.pallas_call(
        functools.partial(_prompt_attn_kernel, tq=tq, cg=cg),
        grid=(b, nq),
        in_specs=[pl.BlockSpec((1, 1, QK_WIDTH, cols), lambda bi, qi: (bi, qi, 0, 0)),
                  pl.BlockSpec((1, l, QK_WIDTH), lambda bi, qi: (bi, 0, 0)),
                  pl.BlockSpec((nq, KV_LORA, tq), lambda bi, qi: (bi, 0, 0)),
                  pl.BlockSpec(wuv_t.shape, lambda bi, qi: (0, 0, 0))],
        out_specs=pl.BlockSpec((1, tq, ATTN_WIDTH), lambda bi, qi: (bi, qi, 0)),
        out_shape=jax.ShapeDtypeStruct((b, l, ATTN_WIDTH), F32),
        scratch_shapes=[pltpu.VMEM((1, cols), F32), pltpu.VMEM((1, cols), F32),
                        pltpu.VMEM((KV_LORA, cols), F32)],
        compiler_params=_params("parallel", "arbitrary"),
        name="prompt_attn",
    )(qt, kf, vt, wuv_t)


def _paged_attn_kernel(pt_ref, q_ref, ks_ref, rs_ref, wuv_ref, ckv_hbm, kr_hbm, o_ref,
                       m_ref, l_ref, acc_ref, kbuf, rbuf, sem, *, pps, page, t):
    b = pl.program_id(0)
    j = pl.program_id(1)
    nj = pl.num_programs(1)
    step = b * nj + j
    slot = step % 2
    rows = N_HEADS * t

    def page_copies(bb, jj, sl):
        for i in range(pps):
            pg = pt_ref[bb, jj * pps + i]
            yield pltpu.make_async_copy(ckv_hbm.at[pg], kbuf.at[sl, i], sem.at[0, sl])
            yield pltpu.make_async_copy(kr_hbm.at[pg], rbuf.at[sl, i], sem.at[1, sl])

    @pl.when(step == 0)
    def _():
        for cp in page_copies(b, j, slot):
            cp.start()

    @pl.when(step + 1 < pl.num_programs(0) * nj)
    def _():
        last = j == nj - 1
        for cp in page_copies(jnp.where(last, b + 1, b), jnp.where(last, 0, j + 1), 1 - slot):
            cp.start()

    q = q_ref[0].astype(BF16)
    q_lat = q[:, :KV_LORA]
    q_rope = q[:, KV_LORA:KV_LORA + QK_ROPE]

    @pl.when(j == 0)
    def _():
        _init_softmax(m_ref, l_ref, acc_ref)
        k = jnp.concatenate([ks_ref[0], jnp.zeros((page - t, KV_LORA), F32)], axis=0).astype(BF16)
        r = jnp.concatenate([rs_ref[0], jnp.zeros((page - t, QK_ROPE), F32)], axis=0).astype(BF16)
        s = (_dot_nt(q_lat, k) + _dot_nt(q_rope, r)) * ATTN_SCALE
        tok = lax.broadcasted_iota(jnp.int32, (t, page), 0)
        key = lax.broadcasted_iota(jnp.int32, (t, page), 1)
        s = jnp.where((key <= tok)[None], s.reshape(N_HEADS, t, page), NEG_INF).reshape(rows, page)
        _softmax_step(s, k, m_ref, l_ref, acc_ref)

    for cp in page_copies(b, j, slot):
        cp.wait()
    half = pps // 2
    for lo in (0, half):
        k = kbuf[slot, lo:lo + half].reshape(half * page, KV_LORA).astype(BF16)
        r_t = jnp.concatenate([rbuf[slot, i] for i in range(lo, lo + half)], axis=1).astype(BF16)
        s = (_dot_nt(q_lat, k) + _dot(q_rope, r_t)) * ATTN_SCALE
        _softmax_step(s, k, m_ref, l_ref, acc_ref)

    @pl.when(j == nj - 1)
    def _():
        o_ref[0] = _value_up(acc_ref[...] / l_ref[...], wuv_ref, t)


def _paged_attn(q, ckv_new, kr_new, cache_ckv, cache_kr, page_table, wuv, pps):
    bd, rows, _ = q.shape
    t = rows // N_HEADS
    _, page, _ = cache_ckv.shape
    n_pages = page_table.shape[1]
    assert n_pages % pps == 0 and pps % 2 == 0 and t <= page
    grid_spec = pltpu.PrefetchScalarGridSpec(
        num_scalar_prefetch=1,
        grid=(bd, n_pages // pps),
        in_specs=[pl.BlockSpec((1, rows, QK_WIDTH), lambda b, j, pt: (b, 0, 0)),
                  pl.BlockSpec((1, t, KV_LORA), lambda b, j, pt: (b, 0, 0)),
                  pl.BlockSpec((1, t, QK_ROPE), lambda b, j, pt: (b, 0, 0)),
                  pl.BlockSpec(wuv.shape, lambda b, j, pt: (0, 0, 0)),
                  pl.BlockSpec(memory_space=pl.ANY), pl.BlockSpec(memory_space=pl.ANY)],
        out_specs=pl.BlockSpec((1, t, ATTN_WIDTH), lambda b, j, pt: (b, 0, 0)),
        scratch_shapes=[pltpu.VMEM((rows, 1), F32), pltpu.VMEM((rows, 1), F32),
                        pltpu.VMEM((rows, KV_LORA), F32),
                        pltpu.VMEM((2, pps, page, KV_LORA), F32), pltpu.VMEM((2, pps, QK_ROPE, page), F32),
                        pltpu.SemaphoreType.DMA((2, 2))],
    )
    return pl.pallas_call(
        functools.partial(_paged_attn_kernel, pps=pps, page=page, t=t),
        grid_spec=grid_spec,
        out_shape=jax.ShapeDtypeStruct((bd, t, ATTN_WIDTH), F32),
        compiler_params=_params("arbitrary", "arbitrary"),
        name="paged_attn",
    )(page_table, q, ckv_new, kr_new, wuv, cache_ckv, cache_kr.transpose(0, 2, 1))


SET_GROUPS = LANE // SSM_GROUP
N_SETS = SSM_GROUPS // SET_GROUPS
SET_STATE = SET_GROUPS * SSM_STATE


def _cmul(xr, xi, ar, ai):
    return xr * ar - xi * ai, xr * ai + xi * ar


def _s5_kernel(*refs, tc, nc, has_h0):
    if has_h0:
        u_ref, d_ref, ws_ref, bz_ref, cz_ref, pw_ref, h0_ref, y_ref, hl_ref = refs
    else:
        u_ref, d_ref, ws_ref, bz_ref, cz_ref, pw_ref, y_ref, hl_ref = refs
    rows = u_ref.shape[0] // tc
    u_step = [u_ref[pl.ds(s, rows, stride=tc), :] for s in range(tc)]
    u_cat = jnp.concatenate([u.astype(BF16) for u in u_step], axis=1)
    z = _dot(u_cat, bz_ref[0])
    sr, si = z[:, :SET_STATE], z[:, SET_STATE:]
    if has_h0:
        assert nc == 1
        pr, pi = h0_ref[0, 0, :, :SET_STATE], h0_ref[0, 0, :, SET_STATE:]
        dr, di = _cmul(pr, pi, pw_ref[0, 0, 0:1], pw_ref[0, 0, 1:2])
        sr, si = sr + dr, si + di
    else:
        chunk = lax.broadcasted_iota(jnp.int32, (rows, SET_STATE), 0) & (nc - 1)
        for k in range(nc.bit_length() - 1):
            sh = 1 << k
            keep = chunk >= sh
            tr = jnp.where(keep, pltpu.roll(sr, sh, axis=0), 0.0)
            ti = jnp.where(keep, pltpu.roll(si, sh, axis=0), 0.0)
            dr, di = _cmul(tr, ti, pw_ref[0, k, 0:1], pw_ref[0, k, 1:2])
            sr, si = sr + dr, si + di
        pr = jnp.where(chunk >= 1, pltpu.roll(sr, 1, axis=0), 0.0)
        pi = jnp.where(chunk >= 1, pltpu.roll(si, 1, axis=0), 0.0)
    y_state = _dot(jnp.concatenate([pr, pi], axis=1).astype(BF16), cz_ref[0])
    for t in range(tc):
        y_t = y_state[:, t * LANE:(t + 1) * LANE] + _dot(u_cat[:, :(t + 1) * LANE], ws_ref[0, (tc - 1 - t) * LANE:, :])
        y_ref[pl.ds(t, rows, stride=tc), :] = y_t + d_ref[...] * u_step[t]
    s_last = jnp.concatenate([sr, si], axis=1)
    if nc == 1:
        hl_ref[0, 0] = s_last
    else:
        for q in range(rows // nc):
            hl_ref[0, 0, q:q + 1, :] = s_last[q * nc + nc - 1:q * nc + nc, :]


def _s5(u, d, ws, bz, cz, pw, h0, tc, nc, block_rows):
    n = u.shape[0]
    nblk = n // block_rows
    nseq = block_rows // (tc * nc)
    per_set = lambda a: pl.BlockSpec((1,) + a.shape[1:], lambda si, rb: (si,) + (0,) * (a.ndim - 1))
    tok = pl.BlockSpec((block_rows, LANE), lambda si, rb: (rb, si))
    state = pl.BlockSpec((1, 1, nseq, 2 * SET_STATE), lambda si, rb: (si, rb, 0, 0))
    skip = pl.BlockSpec((1, LANE), lambda si, rb: (0, si))
    ins = [u, d, ws, bz, cz, pw] + ([h0] if h0 is not None else [])
    return pl.pallas_call(
        functools.partial(_s5_kernel, tc=tc, nc=nc, has_h0=h0 is not None),
        grid=(N_SETS, nblk),
        in_specs=[tok, skip, per_set(ws), per_set(bz), per_set(cz), per_set(pw)]
                 + ([state] if h0 is not None else []),
        out_specs=[tok, state],
        out_shape=[jax.ShapeDtypeStruct((n, SSM_GROUPS * SSM_GROUP), F32),
                   jax.ShapeDtypeStruct((N_SETS, nblk, nseq, 2 * SET_STATE), F32)],
        compiler_params=_params("parallel", "parallel"),
        name="s5",
    )(*ins)


def _block_diag(x):
    a, b = x.shape[-2:]
    rows = x.reshape(x.shape[:-3] + (SET_GROUPS * a, b)).astype(BF16)
    spread = lax.broadcasted_iota(jnp.int32, (b, SET_GROUPS * b), 1) % b == lax.broadcasted_iota(
        jnp.int32, (b, SET_GROUPS * b), 0)
    wide = jnp.dot(rows, spread.astype(BF16), preferred_element_type=BF16)
    shape = (SET_GROUPS * a, SET_GROUPS * b)
    same = lax.broadcasted_iota(jnp.int32, shape, 0) // a == lax.broadcasted_iota(jnp.int32, shape, 1) // b
    return jnp.where(same, wide, jnp.zeros((), BF16))


def _s5_weights(lam_re, lam_im, log_dt, b_re, b_im, c_re, c_im, tc, n_levels):
    hi = lax.Precision.HIGHEST
    lam = lax.complex(lam_re.astype(F32), lam_im.astype(F32))
    ldt = lam * jnp.exp(log_dt.astype(F32))[:, None]
    lam_bar = jnp.exp(ldt)
    b_bar = ((lam_bar - 1.0) / lam)[..., None] * lax.complex(b_re.astype(F32), b_im.astype(F32))
    c = lax.complex(c_re.astype(F32), c_im.astype(F32))
    sets = lambda x: x.reshape(x.shape[:-3] + (N_SETS, SET_GROUPS) + x.shape[-2:])
    steps = jnp.arange(tc + 1, dtype=F32)
    pw = jnp.exp(ldt[None] * steps[:, None, None])
    kern = jnp.real(jnp.einsum('gop,dgp,gpi->dgio', c, pw[:tc], b_bar, precision=hi))
    ws = _block_diag(sets(kern[::-1])).transpose(1, 0, 2, 3).reshape(N_SETS, tc * LANE, LANE)
    bzc = (pw[tc - 1 - jnp.arange(tc)][..., None] * b_bar[None]).transpose(0, 1, 3, 2)
    bz = jnp.concatenate([_block_diag(sets(jnp.real(bzc))), _block_diag(sets(jnp.imag(bzc)))], axis=-1)
    bz = bz.transpose(1, 0, 2, 3).reshape(N_SETS, tc * LANE, 2 * SET_STATE)
    czc = (c[None] * pw[1:tc + 1][:, :, None, :]).transpose(0, 1, 3, 2)
    cz = jnp.concatenate([_block_diag(sets(jnp.real(czc))), _block_diag(sets(-jnp.imag(czc)))], axis=-2)
    cz = jnp.concatenate([cz[t] for t in range(tc)], axis=-1)
    lev = jnp.exp(ldt[None] * (tc * 2.0 ** jnp.arange(n_levels, dtype=F32))[:, None, None])
    pwl = jnp.stack([jnp.real(lev), jnp.imag(lev)], axis=1).reshape(n_levels, 2, N_SETS, SET_STATE)
    return ws, bz, cz, pwl.transpose(2, 0, 1, 3)


def _s5_branch(u, h0, ssm_w, d, batch, seq, tc):
    nc = seq // tc
    n_levels = max(nc.bit_length() - 1, 1)
    assert nc & (nc - 1) == 0 and (h0 is None or nc == 1)
    ws, bz, cz, pwl = _s5_weights(*ssm_w, tc, n_levels)
    block_rows = seq if nc > 1 else batch * seq
    nseq = block_rows // seq
    h0_t = None
    if h0 is not None:
        h0_t = h0.astype(F32).reshape(batch // nseq, nseq, N_SETS, SET_GROUPS, SSM_STATE, 2)
        h0_t = h0_t.transpose(2, 0, 1, 5, 3, 4).reshape(N_SETS, batch // nseq, nseq, 2 * SET_STATE)
    y, hl = _s5(u, d, ws, bz, cz, pwl, h0_t, tc, nc, block_rows)
    hl = hl.reshape(N_SETS, batch, 2, SET_GROUPS, SSM_STATE).transpose(1, 0, 3, 4, 2)
    return y, hl.reshape(batch, SSM_GROUPS, SSM_STATE, 2)


def _merge_kernel(x_ref, y_ref, gs_ref, o_ref, ga_ref, ms_ref, ma_ref,
                  wglu_ref, bglu_ref, wbs_ref, wba_ref, wout_ref, gf_ref, out_ref):
    gate = lambda ref: ref[...].astype(F32)
    zg = jax.nn.gelu(y_ref[...])
    glu = zg * jax.nn.sigmoid(_dot(zg.astype(BF16), wglu_ref[...]) + bglu_ref[...])
    y_s = _dot((glu * jax.nn.silu(gate(gs_ref))).astype(BF16), wbs_ref[...])
    y_a = _dot((o_ref[...] * jax.nn.silu(gate(ga_ref))).astype(BF16), wba_ref[...])
    merged = jax.nn.sigmoid(gate(ms_ref)) * y_s + jax.nn.sigmoid(gate(ma_ref)) * y_a
    h = x_ref[...] + _dot(merged.astype(BF16), wout_ref[...])
    out_ref[...] = _rms(h, gf_ref[...])


def _merge(x, y, gs, o, ga, ms, ma, wglu, bglu, wbs, wba, wout, gf, tm):
    n = x.shape[0]
    row = pl.BlockSpec((tm, D_MODEL), lambda i: (i, 0))
    full = lambda a: pl.BlockSpec(a.shape, lambda i: (0,) * a.ndim)
    consts = (wglu, bglu, wbs, wba, wout, gf)
    return pl.pallas_call(
        _merge_kernel,
        grid=(n // tm,),
        in_specs=[row] * 7 + [full(a) for a in consts],
        out_specs=row,
        out_shape=jax.ShapeDtypeStruct((n, D_MODEL), F32),
        compiler_params=_params("parallel"),
        name="merge",
    )(x, y, gs, o, ga, ms, ma, *consts)


def _rope_tables(pos, reps):
    half = QK_ROPE // 2
    inv = ROPE_BASE ** (-jnp.arange(half, dtype=F32) * (2.0 / QK_ROPE))
    ang = pos.astype(F32)[:, None] * inv[None, :]
    cos, sin = jnp.cos(ang), jnp.sin(ang)
    cos_t = _pad_last(jnp.concatenate([cos, cos], axis=-1), ROPE_PAD)
    sin_t = _pad_last(jnp.concatenate([-sin, sin], axis=-1), ROPE_PAD)
    return jnp.tile(cos_t, (reps, 1)), jnp.tile(sin_t, (reps, 1))


def _swap_halves(w):
    half = QK_ROPE // 2
    return jnp.concatenate([w[..., half:], w[..., :half]], axis=-1)


def _pad_last(w, width):
    return jnp.pad(w, [(0, 0)] * (w.ndim - 1) + [(0, width - w.shape[-1])])


def _rope_weight(w_rope):
    return _pad_last(w_rope, ROPE_PAD).reshape(Q_LORA, N_HEADS * ROPE_PAD).astype(BF16)


def _rope_weight_t(w_rope):
    return w_rope.transpose(1, 2, 0).reshape(N_HEADS * QK_ROPE, Q_LORA).astype(BF16)


def _pack_weights(w_in, mla_w_uq, mla_w_uk, mla_w_uv):
    b = [0]
    for w in (D_MODEL, D_MODEL, Q_LORA, KV_LORA, QK_ROPE, ATTN_WIDTH, D_MODEL, D_MODEL):
        b.append(b[-1] + w)
    w_kr = w_in[:, b[4]:b[5]]
    w_all = jnp.concatenate([w_in[:, :b[4]], _pad_last(w_kr, ROPE_PAD), _pad_last(_swap_halves(w_kr), ROPE_PAD),
                             w_in[:, b[5]:]], axis=-1).astype(BF16)
    w_nope = _pad_last(mla_w_uq[..., :QK_NOPE], LANE).reshape(Q_LORA, N_HEADS * LANE).astype(BF16)
    w_rope = mla_w_uq[..., QK_NOPE:]
    w_uk = jnp.pad(mla_w_uk.transpose(1, 2, 0), ((0, 0), (0, LANE - QK_NOPE), (0, 0))).astype(BF16)
    w_uv = mla_w_uv.transpose(1, 0, 2)
    lo = jnp.pad(w_uv, ((0, 0), (0, 0), (0, LANE - V_DIM)))
    hi = jnp.pad(w_uv, ((0, 0), (0, 0), (LANE - V_DIM, 0)))
    w_uv_pair = jnp.where((jnp.arange(N_HEADS) % 2 == 0)[:, None, None], lo, hi).astype(BF16)
    return dict(
        w_all=w_all, w_nope=w_nope, w_uk=w_uk, w_uv_pair=w_uv_pair,
        w_r=_rope_weight(w_rope), w_rs=_rope_weight(_swap_halves(w_rope)),
        w_r_t=_rope_weight_t(w_rope), w_rs_t=_rope_weight_t(_swap_halves(w_rope)),
        w_uv_t=w_uv.transpose(0, 2, 1).astype(BF16))


def _layer(x, pos, h0, attend, w, batch, seq, tc, tm):
    n = x.shape[0]
    cos_t, sin_t = _rope_tables(pos, max(tm // seq, 1))
    row = lambda v: v.reshape(1, -1).astype(F32)
    u, gs, cq, ckv, kr, kf, vt, ga, ms, ma = _in_proj(x, row(w["norm_in"]), w["w_all"], row(w["mla_kv_norm"]),
                                                      cos_t, sin_t, tm)
    o = attend(cq, cos_t, sin_t, ckv, kr, kf, vt).reshape(n, ATTN_WIDTH)
    y_ssm, state = _s5_branch(u, h0, w["ssm"], row(w["ssm_d"]), batch, seq, tc)
    y = _merge(x, y_ssm, gs, o, ga, ms, ma, w["ssm_w_glu"].astype(BF16), row(w["ssm_b_glu"]),
               w["w_br_ssm"].astype(BF16), w["w_br_attn"].astype(BF16), w["w_out"].astype(BF16),
               row(w["norm_final"]), tm)
    return y, ckv, kr, state


def kernel(x_prompt, x_sample, cache_ckv, cache_krope, state_ssm, page_table,
           norm_in, w_in, ssm_lambda_re, ssm_lambda_im, ssm_log_dt, ssm_b_re, ssm_b_im,
           ssm_c_re, ssm_c_im, ssm_d, ssm_w_glu, ssm_b_glu, w_br_ssm,
           mla_q_norm, mla_w_uq, mla_kv_norm, mla_w_uk, mla_w_uv, w_br_attn, w_out, norm_final):
    b, l, _ = x_prompt.shape
    bd, t, _ = x_sample.shape
    w = dict(norm_in=norm_in, mla_kv_norm=mla_kv_norm,
             ssm=(ssm_lambda_re, ssm_lambda_im, ssm_log_dt, ssm_b_re, ssm_b_im, ssm_c_re, ssm_c_im),
             ssm_d=ssm_d, ssm_w_glu=ssm_w_glu, ssm_b_glu=ssm_b_glu, w_br_ssm=w_br_ssm,
             w_br_attn=w_br_attn, w_out=w_out, norm_final=norm_final,
             **_pack_weights(w_in, mla_w_uq, mla_w_uk, mla_w_uv))
    tm = 256
    gq = mla_q_norm.reshape(1, -1).astype(F32)

    def attend_prompt(cq, cos_t, sin_t, ckv, kr, kf, vt):
        qt = _q_prep_t(cq, gq, w["w_nope"].T, w["w_r_t"], w["w_rs_t"], w["w_uk"].transpose(0, 2, 1),
                       cos_t.T[:QK_ROPE], sin_t.T[:QK_ROPE], tm, b, l)
        return _prompt_attn(qt, kf.reshape(b, l, QK_WIDTH), vt, w["w_uv_t"], tm, cg=4096)

    def attend_sample(cq, cos_t, sin_t, ckv, kr, kf, vt):
        q = _q_prep(cq, gq, w["w_nope"], w["w_r"], w["w_rs"], w["w_uk"], cos_t, sin_t, tm, bd, t, F32)
        return _paged_attn(q.reshape(bd, N_HEADS * t, QK_WIDTH), ckv.reshape(bd, t, KV_LORA),
                           kr.reshape(bd, t, QK_ROPE), cache_ckv, cache_krope, page_table, w["w_uv_pair"],
                           pps=128)

    y_p, ckv_p, kr_p, ssm_p = _layer(x_prompt.reshape(b * l, D_MODEL), jnp.arange(l), None, attend_prompt,
                                     w, b, l, 16, tm)
    y_s, ckv_s, kr_s, ssm_s = _layer(x_sample.reshape(bd * t, D_MODEL), PAST_LEN + jnp.arange(t), state_ssm,
                                     attend_sample, w, bd, t, t, tm)
    return (y_p.reshape(b, l, D_MODEL), y_s.reshape(bd, t, D_MODEL),
            ckv_p.reshape(b, l, KV_LORA), kr_p.reshape(b, l, QK_ROPE), ssm_p,
            ckv_s.reshape(bd, t, KV_LORA), kr_s.reshape(bd, t, QK_ROPE), ssm_s)
```
